```python
import math
import jax, jax.numpy as jnp
from jax import lax
import numpy as np

D_MODEL = 2048
BATCH = 4
SEQ = 4096
DEPTH = 2

N_MIXERS = 2
N_MLA_LAYERS = (DEPTH + 1) // 2
N_GDN_LAYERS = DEPTH // 2

MLA_HEADS = 16
MLA_Q_RANK = 512
MLA_KV_RANK = 512
MLA_NOPE_DIM = 128
MLA_ROPE_DIM = 64
MLA_V_DIM = 128
MLA_IN_DIM = MLA_Q_RANK + MLA_KV_RANK + MLA_ROPE_DIM
MLA_SCALE = (MLA_NOPE_DIM + MLA_ROPE_DIM) ** -0.5
ROPE_THETA = 10000.0
Q_BLOCK = 128

GDN_K_HEADS = 16
GDN_V_HEADS = 32
GDN_K_DIM = 128
GDN_V_DIM = 128
CONV_K = 4
CHUNK = 64
GDN_QK_WIDTH = GDN_K_HEADS * GDN_K_DIM
GDN_V_WIDTH = GDN_V_HEADS * GDN_V_DIM
GDN_CONV_DIM = 2 * GDN_QK_WIDTH + GDN_V_WIDTH
GDN_IN_DIM = GDN_CONV_DIM + GDN_V_WIDTH + 2 * GDN_V_HEADS

N_EXPERTS = 64
TOP_K = 8
N_GROUPS = 8
TOPK_GROUPS = 4
EXPERT_DIM = 512
SHARED_DIM = 512
ROUTED_SCALE = 2.5
MOE_BLOCK = 256

ALPHA = (2 * DEPTH) ** 0.25
BETA = (8 * DEPTH) ** -0.25
LN_EPS = 1e-5
RMS_EPS = 1e-6

kernel_name = 'hybrid_mla_gdn_moe_deepnorm'


def layer_norm(x, g, b):
    xf = x.astype(jnp.float32)
    mu = jnp.mean(xf, axis=-1, keepdims=True)
    xc = xf - mu
    var = jnp.mean(xc * xc, axis=-1, keepdims=True)
    y = xc * lax.rsqrt(var + LN_EPS) * g.astype(jnp.float32) + b.astype(jnp.float32)
    return y.astype(x.dtype)


def rms_norm(x, g):
    xf = x.astype(jnp.float32)
    y = xf * lax.rsqrt(jnp.mean(xf * xf, axis=-1, keepdims=True) + RMS_EPS)
    return (y * g.astype(jnp.float32)).astype(x.dtype)


def l2_normalize(x):
    xf = x.astype(jnp.float32)
    return xf * lax.rsqrt(jnp.sum(xf * xf, axis=-1, keepdims=True) + RMS_EPS)


def rope_angles(positions):
    inv_freq = 1.0 / (ROPE_THETA ** (jnp.arange(0, MLA_ROPE_DIM, 2, dtype=jnp.float32) / MLA_ROPE_DIM))
    ang = positions.astype(jnp.float32)[..., None] * inv_freq
    return jnp.cos(ang), jnp.sin(ang)


def apply_rope(x, cos, sin):
    half = x.shape[-1] // 2
    x1 = x[..., :half].astype(jnp.float32)
    x2 = x[..., half:].astype(jnp.float32)
    return jnp.concatenate([x1 * cos - x2 * sin, x2 * cos + x1 * sin], axis=-1).astype(x.dtype)


def causal_block_attention(q_nope, q_rope, k_nope, k_rope, v):
    B, S, H, _ = q_nope.shape
    nqb = S // Q_BLOCK
    qn = jnp.swapaxes(q_nope.reshape(B, nqb, Q_BLOCK, H, MLA_NOPE_DIM), 0, 1)
    qr = jnp.swapaxes(q_rope.reshape(B, nqb, Q_BLOCK, H, MLA_ROPE_DIM), 0, 1)
    k_pos = jnp.arange(S, dtype=jnp.int32)

    def one_block(args):
        qn_b, qr_b, start = args
        s = (jnp.einsum('bqhd,bkhd->bhqk', qn_b, k_nope)
             + jnp.einsum('bqhr,bkr->bhqk', qr_b, k_rope)).astype(jnp.float32) * MLA_SCALE
        q_pos = start + jnp.arange(Q_BLOCK, dtype=jnp.int32)
        mask = q_pos[:, None] >= k_pos[None, :]
        p = jax.nn.softmax(jnp.where(mask, s, -jnp.inf), axis=-1).astype(v.dtype)
        return jnp.einsum('bhqk,bkhd->bqhd', p, v)

    starts = jnp.arange(nqb, dtype=jnp.int32) * Q_BLOCK
    o = lax.map(one_block, (qn, qr, starts))
    return jnp.swapaxes(o, 0, 1).reshape(B, S, H * MLA_V_DIM)


def mla_mixer(h, positions, w_in, q_norm_g, kv_norm_g, w_qb, w_kvb, w_o):
    B, S, _ = h.shape
    c = h @ w_in
    cq = rms_norm(c[..., :MLA_Q_RANK], q_norm_g)
    ckv = rms_norm(c[..., MLA_Q_RANK:MLA_Q_RANK + MLA_KV_RANK], kv_norm_g)
    k_rope = c[..., MLA_Q_RANK + MLA_KV_RANK:]
    q = (cq @ w_qb).reshape(B, S, MLA_HEADS, MLA_NOPE_DIM + MLA_ROPE_DIM)
    kv = (ckv @ w_kvb).reshape(B, S, MLA_HEADS, MLA_NOPE_DIM + MLA_V_DIM)
    cos, sin = rope_angles(positions)
    q_nope = q[..., :MLA_NOPE_DIM]
    q_rope = apply_rope(q[..., MLA_NOPE_DIM:], cos[:, :, None, :], sin[:, :, None, :])
    k_rope = apply_rope(k_rope, cos, sin)
    k_nope = kv[..., :MLA_NOPE_DIM]
    v = kv[..., MLA_NOPE_DIM:]
    o = causal_block_attention(q_nope, q_rope, k_nope, k_rope, v)
    return o @ w_o


def causal_depthwise_conv(x, w):
    C = x.shape[-1]
    return lax.conv_general_dilated(
        x, w[:, None, :].astype(x.dtype), window_strides=(1,), padding=[(CONV_K - 1, 0)],
        dimension_numbers=('NWC', 'WIO', 'NWC'), feature_group_count=C)


def chunked_gated_delta_rule(q, k, v, g, beta):
    B, S, H, DK = q.shape
    DV = v.shape[-1]
    N = S // CHUNK

    def blocks(t):
        return jnp.swapaxes(t.reshape((B, N, CHUNK, H) + t.shape[3:]), 2, 3)

    q, k, v, g, beta = (blocks(t) for t in (q, k, v, g, beta))
    G = jnp.cumsum(g, axis=-1)
    causal = jnp.tril(jnp.ones((CHUNK, CHUNK), dtype=bool))
    strict = jnp.tril(jnp.ones((CHUNK, CHUNK), dtype=bool), -1)
    decay = jnp.exp(jnp.where(causal, G[..., :, None] - G[..., None, :], -jnp.inf))
    kb = k * beta[..., None]
    a = jnp.where(strict, jnp.einsum('bnhid,bnhjd->bnhij', kb, k) * decay, 0.0)
    rhs = jnp.concatenate([v * beta[..., None], kb * jnp.exp(G)[..., None]], axis=-1)
    sol = lax.linalg.triangular_solve(a, rhs, left_side=True, lower=True, unit_diagonal=True)
    u_intra = sol[..., :DV]
    w = sol[..., DV:]
    qk = jnp.einsum('bnhid,bnhjd->bnhij', q, k) * decay
    q_dec = q * jnp.exp(G)[..., None]
    k_dec = k * jnp.exp(G[..., -1:] - G)[..., None]
    chunk_decay = jnp.exp(G[..., -1])

    def step(state, xs):
        u_i, w_i, qk_i, qd_i, kd_i, cd_i = xs
        u = u_i - jnp.einsum('bhcd,bhde->bhce', w_i, state)
        o = jnp.einsum('bhcd,bhde->bhce', qd_i, state) + jnp.einsum('bhij,bhje->bhie', qk_i, u)
        state = state * cd_i[..., None, None] + jnp.einsum('bhcd,bhce->bhde', kd_i, u)
        return state, o

    xs = tuple(jnp.moveaxis(t, 1, 0) for t in (u_intra, w, qk, q_dec, k_dec, chunk_decay))
    state0 = jnp.zeros((B, H, DK, DV), jnp.float32)
    _, o = lax.scan(step, state0, xs)
    return jnp.transpose(o, (1, 0, 3, 2, 4)).reshape(B, S, H, DV)


def gdn_mixer(h, w_in, conv_w, a_log, dt_bias, norm_g, w_out):
    B, S, _ = h.shape
    p = h @ w_in
    qkv = jax.nn.silu(causal_depthwise_conv(p[..., :GDN_CONV_DIM], conv_w))
    o0 = GDN_CONV_DIM
    z = p[..., o0:o0 + GDN_V_WIDTH].reshape(B, S, GDN_V_HEADS, GDN_V_DIM)
    b = p[..., o0 + GDN_V_WIDTH:o0 + GDN_V_WIDTH + GDN_V_HEADS]
    a = p[..., o0 + GDN_V_WIDTH + GDN_V_HEADS:]
    rep = GDN_V_HEADS // GDN_K_HEADS
    q = qkv[..., :GDN_QK_WIDTH].reshape(B, S, GDN_K_HEADS, GDN_K_DIM)
    k = qkv[..., GDN_QK_WIDTH:2 * GDN_QK_WIDTH].reshape(B, S, GDN_K_HEADS, GDN_K_DIM)
    v = qkv[..., 2 * GDN_QK_WIDTH:].reshape(B, S, GDN_V_HEADS, GDN_V_DIM).astype(jnp.float32)
    q = jnp.repeat(l2_normalize(q), rep, axis=2) * (GDN_K_DIM ** -0.5)
    k = jnp.repeat(l2_normalize(k), rep, axis=2)
    beta = jax.nn.sigmoid(b.astype(jnp.float32))
    g = -jnp.exp(a_log.astype(jnp.float32)) * jax.nn.softplus(a.astype(jnp.float32) + dt_bias.astype(jnp.float32))
    o = chunked_gated_delta_rule(q, k, v, g, beta)
    o = rms_norm(o, norm_g) * jax.nn.silu(z.astype(jnp.float32))
    return o.reshape(B, S, GDN_V_WIDTH).astype(h.dtype) @ w_out


def swiglu(x, w_gate_up, w_down):
    gu = x @ w_gate_up
    f = gu.shape[-1] // 2
    return (jax.nn.silu(gu[..., :f]) * gu[..., f:]) @ w_down


def moe_ffn(h, router_w, router_bias, w_gate_up, w_down, shared_gate_up, shared_down):
    B, S, D = h.shape
    T = B * S
    xt = h.reshape(T, D)
    scores = jax.nn.sigmoid((xt @ router_w).astype(jnp.float32))
    biased = scores + router_bias.astype(jnp.float32)
    grp = biased.reshape(T, N_GROUPS, N_EXPERTS // N_GROUPS)
    grp_score = lax.top_k(grp, 2)[0].sum(-1)
    _, top_groups = lax.top_k(grp_score, TOPK_GROUPS)
    group_mask = jnp.any(top_groups[:, :, None] == jnp.arange(N_GROUPS)[None, None, :], axis=1)
    expert_mask = jnp.repeat(group_mask, N_EXPERTS // N_GROUPS, axis=1)
    _, top_e = lax.top_k(jnp.where(expert_mask, biased, -jnp.inf), TOP_K)
    gate = jnp.take_along_axis(scores, top_e, axis=1)
    gate = gate / (gate.sum(-1, keepdims=True) + 1e-20) * ROUTED_SCALE

    shared = swiglu(xt, shared_gate_up, shared_down)

    n_assign = T * TOP_K
    e_flat = top_e.reshape(n_assign)
    tok_flat = jnp.arange(n_assign, dtype=jnp.int32) // TOP_K
    order = jnp.argsort(e_flat)
    e_sorted = e_flat[order]
    counts = jnp.zeros((N_EXPERTS,), jnp.int32).at[e_flat].add(1)
    starts = jnp.cumsum(counts) - counts
    padded = (counts + MOE_BLOCK - 1) // MOE_BLOCK * MOE_BLOCK
    pad_ends = jnp.cumsum(padded)
    pad_starts = pad_ends - padded
    dest = pad_starts[e_sorted] + jnp.arange(n_assign, dtype=jnp.int32) - starts[e_sorted]
    n_blocks = -(-n_assign // MOE_BLOCK) + N_EXPERTS
    n_rows = n_blocks * MOE_BLOCK
    row_tok = jnp.zeros((n_rows,), jnp.int32).at[dest].set(tok_flat[order])
    row_gate = jnp.zeros((n_rows,), jnp.float32).at[dest].set(gate.reshape(n_assign)[order])
    blk_expert = jnp.minimum(
        jnp.searchsorted(pad_ends, jnp.arange(n_blocks, dtype=jnp.int32) * MOE_BLOCK, side='right'),
        N_EXPERTS - 1)

    def expert_block(acc, blk):
        toks, gts, e = blk
        yb = swiglu(xt[toks], w_gate_up[e], w_down[e])
        return acc.at[toks].add(yb * gts[:, None].astype(yb.dtype)), None

    out, _ = lax.scan(expert_block, shared,
                      (row_tok.reshape(n_blocks, MOE_BLOCK), row_gate.reshape(n_blocks, MOE_BLOCK), blk_expert))
    return out.reshape(B, S, D)


def setup_inputs(seed: int = 0) -> dict:
    key = jax.random.key(seed)
    ks = iter(jax.random.split(key, 40))
    f32 = jnp.float32

    def dense(shape, fan_in, scale=1.0):
        return jax.random.normal(next(ks), shape, f32) * (scale * fan_in ** -0.5)

    def gain(shape):
        return 1.0 + 0.02 * jax.random.normal(next(ks), shape, f32)

    def bias(shape):
        return 0.02 * jax.random.normal(next(ks), shape, f32)

    NA, NG = N_MLA_LAYERS, N_GDN_LAYERS
    x = jax.random.normal(next(ks), (BATCH, SEQ, D_MODEL), f32)
    offsets = jax.random.randint(next(ks), (BATCH, 1), 0, 1024, dtype=jnp.int32)
    positions = offsets + jnp.arange(SEQ, dtype=jnp.int32)[None, :]
    dt = jnp.exp(jax.random.uniform(next(ks), (NG, GDN_V_HEADS), f32, math.log(1e-3), math.log(1e-1)))
    return {
        'x': x,
        'positions': positions,
        'mla_w_in': dense((NA, D_MODEL, MLA_IN_DIM), D_MODEL),
        'mla_q_norm': gain((NA, MLA_Q_RANK)),
        'mla_kv_norm': gain((NA, MLA_KV_RANK)),
        'mla_w_qb': dense((NA, MLA_Q_RANK, MLA_HEADS * (MLA_NOPE_DIM + MLA_ROPE_DIM)), MLA_Q_RANK),
        'mla_w_kvb': dense((NA, MLA_KV_RANK, MLA_HEADS * (MLA_NOPE_DIM + MLA_V_DIM)), MLA_KV_RANK),
        'mla_w_o': dense((NA, MLA_HEADS * MLA_V_DIM, D_MODEL), MLA_HEADS * MLA_V_DIM, BETA),
        'gdn_w_in': dense((NG, D_MODEL, GDN_IN_DIM), D_MODEL),
        'gdn_conv_w': dense((NG, CONV_K, GDN_CONV_DIM), CONV_K),
        'gdn_a_log': jnp.log(jax.random.uniform(next(ks), (NG, GDN_V_HEADS), f32, 1.0, 16.0)),
        'gdn_dt_bias': dt + jnp.log(-jnp.expm1(-dt)),
        'gdn_norm': gain((NG, GDN_V_DIM)),
        'gdn_w_out': dense((NG, GDN_V_WIDTH, D_MODEL), GDN_V_WIDTH, BETA),
        'ln_mix_g': gain((DEPTH, D_MODEL)),
        'ln_mix_b': bias((DEPTH, D_MODEL)),
        'ln_ffn_g': gain((DEPTH, D_MODEL)),
        'ln_ffn_b': bias((DEPTH, D_MODEL)),
        'moe_router': dense((DEPTH, D_MODEL, N_EXPERTS), D_MODEL),
        'moe_router_bias': 0.01 * jax.random.normal(next(ks), (DEPTH, N_EXPERTS), f32),
        'moe_w_gate_up': dense((DEPTH, N_EXPERTS, D_MODEL, 2 * EXPERT_DIM), D_MODEL),
        'moe_w_down': dense((DEPTH, N_EXPERTS, EXPERT_DIM, D_MODEL), EXPERT_DIM, BETA),
        'moe_shared_gate_up': dense((DEPTH, D_MODEL, 2 * SHARED_DIM), D_MODEL),
        'moe_shared_down': dense((DEPTH, SHARED_DIM, D_MODEL), SHARED_DIM, BETA),
    }


def reference(x, positions, mla_w_in, mla_q_norm, mla_kv_norm, mla_w_qb, mla_w_kvb, mla_w_o,
              gdn_w_in, gdn_conv_w, gdn_a_log, gdn_dt_bias, gdn_norm, gdn_w_out,
              ln_mix_g, ln_mix_b, ln_ffn_g, ln_ffn_b,
              moe_router, moe_router_bias, moe_w_gate_up, moe_w_down, moe_shared_gate_up, moe_shared_down):
    for i in range(DEPTH):
        j = i // N_MIXERS
        if i % N_MIXERS == 0:
            mix = mla_mixer(x, positions, mla_w_in[j], mla_q_norm[j], mla_kv_norm[j],
                            mla_w_qb[j], mla_w_kvb[j], mla_w_o[j])
        else:
            mix = gdn_mixer(x, gdn_w_in[j], gdn_conv_w[j], gdn_a_log[j], gdn_dt_bias[j],
                            gdn_norm[j], gdn_w_out[j])
        x = layer_norm(ALPHA * x + mix, ln_mix_g[i], ln_mix_b[i])
        ffn = moe_ffn(x, moe_router[i], moe_router_bias[i], moe_w_gate_up[i], moe_w_down[i],
                      moe_shared_gate_up[i], moe_shared_down[i])
        x = layer_norm(ALPHA * x + ffn, ln_ffn_g[i], ln_ffn_b[i])
    return x
```

```python
import functools
import math

import jax
import jax.numpy as jnp
from jax import lax
from jax.experimental import pallas as pl
from jax.experimental.pallas import tpu as pltpu

F32 = jnp.float32
BF16 = jnp.bfloat16

DEPTH = 2
MLA_HEADS = 16
MLA_Q_RANK = 512
MLA_KV_RANK = 512
MLA_NOPE_DIM = 128
MLA_ROPE_DIM = 64
MLA_V_DIM = 128
MLA_QK_DIM = MLA_NOPE_DIM + MLA_ROPE_DIM
MLA_SCALE = MLA_QK_DIM ** -0.5
ROPE_THETA = 10000.0

GDN_K_HEADS = 16
GDN_V_HEADS = 32
GDN_K_DIM = 128
GDN_V_DIM = 128
CONV_K = 4
CHUNK = 64
GDN_QK_WIDTH = GDN_K_HEADS * GDN_K_DIM
GDN_V_WIDTH = GDN_V_HEADS * GDN_V_DIM
GDN_CONV_DIM = 2 * GDN_QK_WIDTH + GDN_V_WIDTH

N_EXPERTS = 64
TOP_K = 8
N_GROUPS = 8
TOPK_GROUPS = 4
EXPERT_DIM = 512
ROUTED_SCALE = 2.5
MOE_BLOCK = 256

ALPHA = (2 * DEPTH) ** 0.25
LN_EPS = 1e-5
RMS_EPS = 1e-6

V7X_VMEM_BYTES = 64 * 1024 * 1024
VMEM_LIMIT_CAP = 56 * 1024 * 1024
NEG_BIG = -1e30


def _vmem_limit(nbytes):
    return int(min(max(nbytes, 16 * 1024 * 1024), VMEM_LIMIT_CAP))


def _nbytes(shape, dtype):
    return math.prod(shape) * jnp.dtype(dtype).itemsize


def _mm_kernel(x_ref, w_ref, o_ref):
    o_ref[...] = jnp.dot(x_ref[...], w_ref[...], preferred_element_type=F32).astype(o_ref.dtype)


def _matmul(x, w, out_dtype, tm=1024, tn=1024, name="matmul"):
    M, K = x.shape
    N = w.shape[1]
    tm = min(tm, M)
    tn = min(tn, N)
    assert M % tm == 0 and N % tn == 0, (M, N, tm, tn)
    need = 2 * (_nbytes((tm, K), x.dtype) + _nbytes((K, tn), w.dtype) + _nbytes((tm, tn), out_dtype))
    need += _nbytes((tm, tn), F32)
    return pl.pallas_call(
        _mm_kernel,
        grid=(N // tn, M // tm),
        in_specs=[pl.BlockSpec((tm, K), lambda j, i: (i, 0)),
                  pl.BlockSpec((K, tn), lambda j, i: (0, j))],
        out_specs=pl.BlockSpec((tm, tn), lambda j, i: (i, j)),
        out_shape=jax.ShapeDtypeStruct((M, N), out_dtype),
        compiler_params=pltpu.CompilerParams(
            dimension_semantics=("parallel", "parallel"),
            vmem_limit_bytes=_vmem_limit(need + (4 << 20))),
        name=name,
    )(x, w)


def _attn_kernel(q_ref, k_ref, v_ref, o_ref, *, blk):
    i = pl.program_id(2)
    q = q_ref[0, 0]

    def scores(off):
        k = k_ref[0, 0, pl.ds(off, blk), :]
        return lax.dot_general(q, k, (((1,), (1,)), ((), ())), preferred_element_type=F32)

    def update(carry, s, off):
        m, l, acc = carry
        m_new = jnp.maximum(m, jnp.max(s, axis=-1, keepdims=True))
        alpha = jnp.exp(m - m_new)
        p = jnp.exp(s - m_new)
        l = alpha * l + jnp.sum(p, axis=-1, keepdims=True)
        v = v_ref[0, 0, pl.ds(off, blk), :]
        acc = alpha * acc + jnp.dot(p.astype(BF16), v, preferred_element_type=F32)
        return m_new, l, acc

    def body(j, carry):
        off = pl.multiple_of(j * blk, blk)
        return update(carry, scores(off), off)

    init = (jnp.full((blk, 1), NEG_BIG, F32), jnp.zeros((blk, 1), F32),
            jnp.zeros((blk, MLA_V_DIM), F32))
    carry = lax.fori_loop(0, i, body, init)
    off = pl.multiple_of(i * blk, blk)
    row = lax.broadcasted_iota(jnp.int32, (blk, blk), 0)
    col = lax.broadcasted_iota(jnp.int32, (blk, blk), 1)
    s = jnp.where(row >= col, scores(off), NEG_BIG)
    m, l, acc = update(carry, s, off)
    o_ref[0] = (acc / l).astype(o_ref.dtype)


def _attention(q, k, v, blk=512):
    B, H, S, _ = q.shape
    blk = min(blk, S)
    assert S % blk == 0
    need = 2 * (_nbytes((blk, 256), BF16) + _nbytes((S, 256), BF16) + _nbytes((S, 128), BF16)
                + _nbytes((blk, 128), BF16)) + 6 * _nbytes((blk, blk), F32)
    return pl.pallas_call(
        functools.partial(_attn_kernel, blk=blk),
        grid=(B, H, S // blk),
        in_specs=[pl.BlockSpec((1, 1, blk, MLA_QK_DIM), lambda b, h, i: (b, h, i, 0)),
                  pl.BlockSpec((1, 1, S, MLA_QK_DIM), lambda b, h, i: (b, h, 0, 0)),
                  pl.BlockSpec((1, 1, S, MLA_V_DIM), lambda b, h, i: (b, h, 0, 0))],
        out_specs=pl.BlockSpec((1, blk, MLA_V_DIM), lambda b, h, i: (b, i, h)),
        out_shape=jax.ShapeDtypeStruct((B, S, H * MLA_V_DIM), BF16),
        compiler_params=pltpu.CompilerParams(
            dimension_semantics=("parallel", "parallel", "parallel"),
            vmem_limit_bytes=_vmem_limit(need + (8 << 20))),
        name="mla_attention",
    )(q, k, v)


def _inv_unit_lower(a, eye_f):
    p = eye_f - a
    x = a
    n_sq = int(math.log2(CHUNK)) - 1
    for _ in range(n_sq):
        xb = x.astype(BF16)
        x = jnp.dot(xb, xb, preferred_element_type=F32)
        p = p + jnp.dot(p.astype(BF16), x.astype(BF16), preferred_element_type=F32)
    return p


def _gdn_kernel(q_ref, k_ref, v_ref, g_ref, b_ref, o_ref, state_ref, *, n_chunks, k_heads):
    C = CHUNK
    rep = GDN_V_HEADS // GDN_K_HEADS

    @pl.when(pl.program_id(2) == 0)
    def _():
        state_ref[...] = jnp.zeros_like(state_ref)

    row = lax.broadcasted_iota(jnp.int32, (C, C), 0)
    col = lax.broadcasted_iota(jnp.int32, (C, C), 1)
    causal = row >= col
    strict = row > col
    eye = row == col
    eye_f = eye.astype(F32)
    nt = (((1,), (1,)), ((), ()))
    tn = (((0,), (0,)), ((), ()))

    def chunk(c, carry):
        r0 = pl.multiple_of(c * C, C)
        for kh in range(k_heads):
            q = q_ref[0, pl.ds(r0, C), kh * GDN_K_DIM:(kh + 1) * GDN_K_DIM]
            k = k_ref[0, pl.ds(r0, C), kh * GDN_K_DIM:(kh + 1) * GDN_K_DIM]
            q16 = q.astype(BF16)
            k16 = k.astype(BF16)
            kk = lax.dot_general(k16, k16, nt, preferred_element_type=F32)
            qk = lax.dot_general(q16, k16, nt, preferred_element_type=F32)
            for r in range(rep):
                h = kh * rep + r
                g_row = g_ref[0, h, pl.ds(c, 1), :]
                b_row = b_ref[0, h, pl.ds(c, 1), :]
                g_col = jnp.sum(jnp.where(eye, g_row, 0.0), axis=1, keepdims=True)
                b_col = jnp.sum(jnp.where(eye, b_row, 0.0), axis=1, keepdims=True)
                g_last = g_row[:, C - 1:C]
                decay = jnp.exp(jnp.where(causal, g_col - g_row, NEG_BIG))
                a = jnp.where(strict, kk * b_col * decay, 0.0)
                t = _inv_unit_lower(a, eye_f)
                eg = jnp.exp(g_col)
                v = v_ref[0, pl.ds(r0, C), h * GDN_V_DIM:(h + 1) * GDN_V_DIM]
                rhs = jnp.concatenate([v * b_col, k * (b_col * eg)], axis=1)
                sol = jnp.dot(t.astype(BF16), rhs.astype(BF16), preferred_element_type=F32)
                u_intra = sol[:, :GDN_V_DIM]
                w = sol[:, GDN_V_DIM:]
                s = state_ref[h]
                wq = jnp.concatenate([w, q * eg], axis=0).astype(BF16)
                ws = jnp.dot(wq, s.astype(BF16), preferred_element_type=F32)
                u = u_intra - ws[:C]
                u16 = u.astype(BF16)
                o = ws[C:] + jnp.dot((qk * decay).astype(BF16), u16, preferred_element_type=F32)
                k_dec = (k * jnp.exp(g_last - g_col)).astype(BF16)
                state_ref[h] = s * jnp.exp(g_last) + lax.dot_general(
                    k_dec, u16, tn, preferred_element_type=F32)
                o_ref[0, pl.ds(r0, C), h * GDN_V_DIM:(h + 1) * GDN_V_DIM] = o
        return carry

    lax.fori_loop(0, n_chunks, chunk, 0)


def _gated_delta_rule(q, k, v, g_cum, beta, n_chunks=8, k_heads=4):
    B, S, _ = q.shape
    n_chunks = min(n_chunks, S // CHUNK)
    sb = n_chunks * CHUNK
    rep = GDN_V_HEADS // GDN_K_HEADS
    v_heads = k_heads * rep
    assert S % sb == 0 and GDN_K_HEADS % k_heads == 0
    return pl.pallas_call(
        functools.partial(_gdn_kernel, n_chunks=n_chunks, k_heads=k_heads),
        grid=(B, GDN_K_HEADS // k_heads, S // sb),
        in_specs=[pl.BlockSpec((1, sb, k_heads * GDN_K_DIM), lambda b, h, s: (b, s, h)),
                  pl.BlockSpec((1, sb, k_heads * GDN_K_DIM), lambda b, h, s: (b, s, h)),
                  pl.BlockSpec((1, sb, v_heads * GDN_V_DIM), lambda b, h, s: (b, s, h)),
                  pl.BlockSpec((1, v_heads, n_chunks, CHUNK), lambda b, h, s: (b, h, s, 0)),
                  pl.BlockSpec((1, v_heads, n_chunks, CHUNK), lambda b, h, s: (b, h, s, 0))],
        out_specs=pl.BlockSpec((1, sb, v_heads * GDN_V_DIM), lambda b, h, s: (b, s, h)),
        out_shape=jax.ShapeDtypeStruct((B, S, GDN_V_WIDTH), F32),
        scratch_shapes=[pltpu.VMEM((v_heads, GDN_K_DIM, GDN_V_DIM), F32)],
        compiler_params=pltpu.CompilerParams(
            dimension_semantics=("parallel", "parallel", "arbitrary"),
            vmem_limit_bytes=_vmem_limit(32 << 20)),
        name="gated_delta_rule",
    )(q, k, v, g_cum, beta)


def _expert_kernel(be_ref, nu_ref, x_ref, wgu_ref, wd_ref, o_ref):
    i = pl.program_id(0)

    @pl.when(i < nu_ref[0])
    def _():
        gu = jnp.dot(x_ref[...], wgu_ref[0], preferred_element_type=F32)
        f = gu.shape[-1] // 2
        gate = gu[:, :f]
        act = gate * jax.nn.sigmoid(gate) * gu[:, f:]
        o_ref[...] = jnp.dot(act.astype(BF16), wd_ref[0], preferred_element_type=F32).astype(o_ref.dtype)

    @pl.when(i >= nu_ref[0])
    def _():
        o_ref[...] = jnp.zeros_like(o_ref)


def _grouped_swiglu(x_rows, w_gate_up, w_down, blk_expert, n_used, tm, out_dtype, name):
    R, D = x_rows.shape
    F2 = w_gate_up.shape[-1]
    tm = min(tm, R)
    assert R % tm == 0
    need = 2 * (_nbytes((tm, D), BF16) + _nbytes((D, F2), BF16) + _nbytes((F2 // 2, D), BF16)
                + _nbytes((tm, D), out_dtype)) + _nbytes((tm, F2), F32) * 2 + _nbytes((tm, D), F32)
    grid_spec = pltpu.PrefetchScalarGridSpec(
        num_scalar_prefetch=2,
        grid=(R // tm,),
        in_specs=[pl.BlockSpec((tm, D), lambda i, be, nu: (i, 0)),
                  pl.BlockSpec((1, D, F2), lambda i, be, nu: (be[i], 0, 0)),
                  pl.BlockSpec((1, F2 // 2, D), lambda i, be, nu: (be[i], 0, 0))],
        out_specs=pl.BlockSpec((tm, D), lambda i, be, nu: (i, 0)),
    )
    return pl.pallas_call(
        _expert_kernel,
        grid_spec=grid_spec,
        out_shape=jax.ShapeDtypeStruct((R, D), out_dtype),
        compiler_params=pltpu.CompilerParams(
            dimension_semantics=("arbitrary",),
            vmem_limit_bytes=_vmem_limit(need + (4 << 20))),
        name=name,
    )(blk_expert, n_used, x_rows, w_gate_up, w_down)


def _layer_norm(x, g, b):
    mu = jnp.mean(x, axis=-1, keepdims=True)
    xc = x - mu
    var = jnp.mean(xc * xc, axis=-1, keepdims=True)
    return xc * lax.rsqrt(var + LN_EPS) * g + b


def _rms_norm(x, g):
    return x * lax.rsqrt(jnp.mean(x * x, axis=-1, keepdims=True) + RMS_EPS) * g


def _rope(x, cos, sin):
    half = x.shape[-1] // 2
    x1, x2 = x[..., :half], x[..., half:]
    return jnp.concatenate([x1 * cos - x2 * sin, x2 * cos + x1 * sin], axis=-1)


def _mla_mixer(x, positions, w_in, q_norm_g, kv_norm_g, w_qb, w_kvb, w_o):
    B, S, D = x.shape
    T = B * S
    H = MLA_HEADS
    c = _matmul(x.reshape(T, D).astype(BF16), w_in.astype(BF16), F32, tn=w_in.shape[1], name="mla_in")
    cq = _rms_norm(c[:, :MLA_Q_RANK], q_norm_g)
    ckv = _rms_norm(c[:, MLA_Q_RANK:MLA_Q_RANK + MLA_KV_RANK], kv_norm_g)
    k_rope = c[:, MLA_Q_RANK + MLA_KV_RANK:].reshape(B, S, MLA_ROPE_DIM)
    q = _matmul(cq.astype(BF16), w_qb.astype(BF16), F32, name="mla_qb").reshape(B, S, H, MLA_QK_DIM)
    kv = _matmul(ckv.astype(BF16), w_kvb.astype(BF16), BF16, name="mla_kvb").reshape(
        B, S, H, MLA_NOPE_DIM + MLA_V_DIM)
    inv_freq = 1.0 / (ROPE_THETA ** (jnp.arange(0, MLA_ROPE_DIM, 2, dtype=F32) / MLA_ROPE_DIM))
    ang = positions.astype(F32)[..., None] * inv_freq
    cos, sin = jnp.cos(ang), jnp.sin(ang)
    q_rope = _rope(q[..., MLA_NOPE_DIM:], cos[:, :, None, :], sin[:, :, None, :])
    k_rope = _rope(k_rope, cos, sin)
    q_cat = jnp.concatenate([q[..., :MLA_NOPE_DIM], q_rope], axis=-1) * MLA_SCALE
    q_cat = jnp.transpose(q_cat, (0, 2, 1, 3)).astype(BF16)
    k_cat = jnp.concatenate(
        [kv[..., :MLA_NOPE_DIM],
         jnp.broadcast_to(k_rope[:, :, None, :].astype(BF16), (B, S, H, MLA_ROPE_DIM))], axis=-1)
    k_cat = jnp.transpose(k_cat, (0, 2, 1, 3))
    v = jnp.transpose(kv[..., MLA_NOPE_DIM:], (0, 2, 1, 3))
    o = _attention(q_cat, k_cat, v)
    return _matmul(o.reshape(T, H * MLA_V_DIM), w_o.astype(BF16), F32, name="mla_out").reshape(B, S, D)


def _gdn_mixer(x, w_in, conv_w, a_log, dt_bias, norm_g, w_out):
    B, S, D = x.shape
    T = B * S
    n_main = GDN_CONV_DIM + GDN_V_WIDTH
    xb = x.reshape(T, D).astype(BF16)
    p = _matmul(xb, w_in[:, :n_main].astype(BF16), F32, name="gdn_in")
    ba = _matmul(xb, w_in[:, n_main:].astype(BF16), F32, name="gdn_in_ba")
    pc = p[:, :GDN_CONV_DIM].reshape(B, S, GDN_CONV_DIM)
    pad = jnp.pad(pc, ((0, 0), (CONV_K - 1, 0), (0, 0)))
    conv = sum(pad[:, j:j + S, :] * conv_w[j] for j in range(CONV_K))
    qkv = jax.nn.silu(conv)
    z = p[:, GDN_CONV_DIM:].reshape(B, S, GDN_V_HEADS, GDN_V_DIM)
    b = ba[:, :GDN_V_HEADS].reshape(B, S, GDN_V_HEADS)
    a = ba[:, GDN_V_HEADS:].reshape(B, S, GDN_V_HEADS)

    def l2n(t):
        t = t.reshape(B, S, GDN_K_HEADS, GDN_K_DIM)
        t = t * lax.rsqrt(jnp.sum(t * t, axis=-1, keepdims=True) + RMS_EPS)
        return t.reshape(B, S, GDN_QK_WIDTH)

    q = l2n(qkv[..., :GDN_QK_WIDTH]) * (GDN_K_DIM ** -0.5)
    k = l2n(qkv[..., GDN_QK_WIDTH:2 * GDN_QK_WIDTH])
    v = qkv[..., 2 * GDN_QK_WIDTH:]
    beta = jax.nn.sigmoid(b)
    g = -jnp.exp(a_log) * jax.nn.softplus(a + dt_bias)
    N = S // CHUNK
    g_cum = jnp.cumsum(g.reshape(B, N, CHUNK, GDN_V_HEADS), axis=2)
    g_cum = jnp.transpose(g_cum, (0, 3, 1, 2))
    beta = jnp.transpose(beta.reshape(B, N, CHUNK, GDN_V_HEADS), (0, 3, 1, 2))
    o = _gated_delta_rule(q, k, v, g_cum, beta).reshape(B, S, GDN_V_HEADS, GDN_V_DIM)
    o = _rms_norm(o, norm_g) * jax.nn.silu(z)
    o = o.reshape(T, GDN_V_WIDTH).astype(BF16)
    return _matmul(o, w_out.astype(BF16), F32, tm=512, name="gdn_out").reshape(B, S, D)


def _moe_ffn(h, router_w, router_bias, w_gate_up, w_down, shared_gate_up, shared_down):
    B, S, D = h.shape
    T = B * S
    xt = h.reshape(T, D)
    xb = xt.astype(BF16)
    logits = jnp.dot(xt, router_w, precision=lax.Precision.HIGHEST)
    scores = jax.nn.sigmoid(logits)
    biased = scores + router_bias
    grp = biased.reshape(T, N_GROUPS, N_EXPERTS // N_GROUPS)
    grp_score = lax.top_k(grp, 2)[0].sum(-1)
    _, top_groups = lax.top_k(grp_score, TOPK_GROUPS)
    group_mask = jnp.any(top_groups[:, :, None] == jnp.arange(N_GROUPS)[None, None, :], axis=1)
    expert_mask = jnp.repeat(group_mask, N_EXPERTS // N_GROUPS, axis=1)
    _, top_e = lax.top_k(jnp.where(expert_mask, biased, -jnp.inf), TOP_K)
    gate = jnp.take_along_axis(scores, top_e, axis=1)
    gate = gate / (gate.sum(-1, keepdims=True) + 1e-20) * ROUTED_SCALE

    zero = jnp.zeros((1,), jnp.int32)
    shared = _grouped_swiglu(xb, shared_gate_up[None].astype(BF16), shared_down[None].astype(BF16),
                             jnp.zeros((T // min(512, T),), jnp.int32), zero + T, 512, F32, "shared_expert")

    n_assign = T * TOP_K
    e_flat = top_e.reshape(n_assign).astype(jnp.int32)
    order = jnp.argsort(e_flat)
    e_sorted = e_flat[order]
    counts = jnp.zeros((N_EXPERTS,), jnp.int32).at[e_flat].add(1)
    starts = jnp.cumsum(counts) - counts
    padded = (counts + MOE_BLOCK - 1) // MOE_BLOCK * MOE_BLOCK
    pad_ends = jnp.cumsum(padded)
    pad_starts = pad_ends - padded
    dest = pad_starts[e_sorted] + jnp.arange(n_assign, dtype=jnp.int32) - starts[e_sorted]
    n_blocks = -(-n_assign // MOE_BLOCK) + N_EXPERTS
    n_rows = n_blocks * MOE_BLOCK
    row_tok = jnp.zeros((n_rows,), jnp.int32).at[dest].set((order // TOP_K).astype(jnp.int32))
    pos = jnp.zeros((n_assign,), jnp.int32).at[order].set(dest)
    blk_expert = jnp.minimum(
        jnp.searchsorted(pad_ends, jnp.arange(n_blocks, dtype=jnp.int32) * MOE_BLOCK, side='right'),
        N_EXPERTS - 1).astype(jnp.int32)
    n_used = (pad_ends[-1:] // MOE_BLOCK).astype(jnp.int32)

    x_rows = jnp.take(xb, row_tok, axis=0)
    y_rows = _grouped_swiglu(x_rows, w_gate_up.astype(BF16), w_down.astype(BF16), blk_expert, n_used,
                             MOE_BLOCK, F32, "routed_experts")
    y_tok = jnp.take(y_rows, pos, axis=0).reshape(T, TOP_K, D)
    out = shared + jnp.sum(y_tok * gate[:, :, None], axis=1)
    return out.reshape(B, S, D)


def kernel(x, positions, mla_w_in, mla_q_norm, mla_kv_norm, mla_w_qb, mla_w_kvb, mla_w_o, gdn_w_in, gdn_conv_w, gdn_a_log, gdn_dt_bias, gdn_norm, gdn_w_out, ln_mix_g, ln_mix_b, ln_ffn_g, ln_ffn_b, moe_router, moe_router_bias, moe_w_gate_up, moe_w_down, moe_shared_gate_up, moe_shared_down):
    for i in range(DEPTH):
        j = i // 2
        if i % 2 == 0:
            mix = _mla_mixer(x, positions, mla_w_in[j], mla_q_norm[j], mla_kv_norm[j],
                             mla_w_qb[j], mla_w_kvb[j], mla_w_o[j])
        else:
            mix = _gdn_mixer(x, gdn_w_in[j], gdn_conv_w[j], gdn_a_log[j], gdn_dt_bias[j],
                             gdn_norm[j], gdn_w_out[j])
        x = _layer_norm(ALPHA * x + mix, ln_mix_g[i], ln_mix_b[i])
        ffn = _moe_ffn(x, moe_router[i], moe_router_bias[i], moe_w_gate_up[i], moe_w_down[i],
                       moe_shared_gate_up[i], moe_shared_down[i])
        x = _layer_norm(ALPHA * x + ffn, ln_ffn_g[i], ln_ffn_b[i])
    return x
```

```python
import functools
import math

import jax
import jax.numpy as jnp
from jax import lax
from jax.experimental import pallas as pl
from jax.experimental.pallas import tpu as pltpu

F32 = jnp.float32
BF16 = jnp.bfloat16

DEPTH = 2
MLA_HEADS = 16
MLA_Q_RANK = 512
MLA_KV_RANK = 512
MLA_NOPE_DIM = 128
MLA_ROPE_DIM = 64
MLA_V_DIM = 128
MLA_QK_DIM = MLA_NOPE_DIM + MLA_ROPE_DIM
MLA_SCALE = MLA_QK_DIM ** -0.5
ROPE_THETA = 10000.0

GDN_K_HEADS = 16
GDN_V_HEADS = 32
GDN_K_DIM = 128
GDN_V_DIM = 128
CONV_K = 4
CHUNK = 64
GDN_QK_WIDTH = GDN_K_HEADS * GDN_K_DIM
GDN_V_WIDTH = GDN_V_HEADS * GDN_V_DIM
GDN_CONV_DIM = 2 * GDN_QK_WIDTH + GDN_V_WIDTH

N_EXPERTS = 64
TOP_K = 8
N_GROUPS = 8
TOPK_GROUPS = 4
EXPERT_DIM = 512
ROUTED_SCALE = 2.5
MOE_BLOCK = 256

ALPHA = (2 * DEPTH) ** 0.25
LN_EPS = 1e-5
RMS_EPS = 1e-6

V7X_VMEM_BYTES = 64 * 1024 * 1024
VMEM_LIMIT_CAP = 56 * 1024 * 1024
NEG_BIG = -1e30


def _vmem_limit(nbytes):
    return int(min(max(nbytes, 16 * 1024 * 1024), VMEM_LIMIT_CAP))


def _nbytes(shape, dtype):
    return math.prod(shape) * jnp.dtype(dtype).itemsize


def _layer_norm_rows(y, g, b):
    mu = jnp.mean(y, axis=-1, keepdims=True)
    yc = y - mu
    var = jnp.mean(yc * yc, axis=-1, keepdims=True)
    return yc * lax.rsqrt(var + LN_EPS) * g + b


def _mm_kernel(x_ref, w_ref, o_ref):
    o_ref[...] = jnp.dot(x_ref[...], w_ref[...], preferred_element_type=F32).astype(o_ref.dtype)


def _matmul(x, w, out_dtype, tm=1024, tn=1024, name="matmul"):
    M, K = x.shape
    N = w.shape[1]
    tm = min(tm, M)
    tn = min(tn, N)
    assert M % tm == 0 and N % tn == 0, (M, N, tm, tn)
    need = 2 * (_nbytes((tm, K), x.dtype) + _nbytes((K, tn), w.dtype) + _nbytes((tm, tn), out_dtype))
    need += _nbytes((tm, tn), F32)
    return pl.pallas_call(
        _mm_kernel,
        grid=(N // tn, M // tm),
        in_specs=[pl.BlockSpec((tm, K), lambda j, i: (i, 0)),
                  pl.BlockSpec((K, tn), lambda j, i: (0, j))],
        out_specs=pl.BlockSpec((tm, tn), lambda j, i: (i, j)),
        out_shape=jax.ShapeDtypeStruct((M, N), out_dtype),
        compiler_params=pltpu.CompilerParams(
            dimension_semantics=("parallel", "parallel"),
            vmem_limit_bytes=_vmem_limit(need + (4 << 20))),
        name=name,
    )(x, w)


def _mm_ln_kernel(x_ref, w_ref, r_ref, g_ref, b_ref, o_ref, o16_ref):
    y = ALPHA * r_ref[...] + jnp.dot(x_ref[...], w_ref[...], preferred_element_type=F32)
    out = _layer_norm_rows(y, g_ref[...], b_ref[...])
    o_ref[...] = out
    o16_ref[...] = out.astype(BF16)


def _matmul_ln(x, w, resid, ln_g, ln_b, tm, name):
    M, K = x.shape
    N = w.shape[1]
    tm = min(tm, M)
    assert M % tm == 0
    need = (_nbytes((K, N), BF16) + 2 * _nbytes((tm, K), BF16) + 4 * _nbytes((tm, N), F32)
            + 2 * _nbytes((tm, N), BF16) + 2 * _nbytes((tm, N), F32))
    return pl.pallas_call(
        _mm_ln_kernel,
        grid=(M // tm,),
        in_specs=[pl.BlockSpec((tm, K), lambda i: (i, 0)),
                  pl.BlockSpec((K, N), lambda i: (0, 0), pipeline_mode=pl.Buffered(1)),
                  pl.BlockSpec((tm, N), lambda i: (i, 0)),
                  pl.BlockSpec((1, N), lambda i: (0, 0)),
                  pl.BlockSpec((1, N), lambda i: (0, 0))],
        out_specs=[pl.BlockSpec((tm, N), lambda i: (i, 0)),
                   pl.BlockSpec((tm, N), lambda i: (i, 0))],
        out_shape=[jax.ShapeDtypeStruct((M, N), F32), jax.ShapeDtypeStruct((M, N), BF16)],
        compiler_params=pltpu.CompilerParams(
            dimension_semantics=("parallel",),
            vmem_limit_bytes=_vmem_limit(need + (4 << 20))),
        name=name,
    )(x, w, resid, ln_g.reshape(1, N), ln_b.reshape(1, N))


def _attn_kernel(q_ref, k_ref, v_ref, o_ref, *, blk):
    i = pl.program_id(2)
    q = q_ref[0, 0]

    def scores(off):
        k = k_ref[0, 0, pl.ds(off, blk), :]
        return lax.dot_general(q, k, (((1,), (1,)), ((), ())), preferred_element_type=F32)

    def update(carry, s, off):
        m, l, acc = carry
        m_new = jnp.maximum(m, jnp.max(s, axis=-1, keepdims=True))
        alpha = jnp.exp(m - m_new)
        p = jnp.exp(s - m_new)
        l = alpha * l + jnp.sum(p, axis=-1, keepdims=True)
        v = v_ref[0, 0, pl.ds(off, blk), :]
        acc = alpha * acc + jnp.dot(p.astype(BF16), v, preferred_element_type=F32)
        return m_new, l, acc

    def body(j, carry):
        off = pl.multiple_of(j * blk, blk)
        return update(carry, scores(off), off)

    init = (jnp.full((blk, 1), NEG_BIG, F32), jnp.zeros((blk, 1), F32),
            jnp.zeros((blk, MLA_V_DIM), F32))
    carry = lax.fori_loop(0, i, body, init)
    off = pl.multiple_of(i * blk, blk)
    row = lax.broadcasted_iota(jnp.int32, (blk, blk), 0)
    col = lax.broadcasted_iota(jnp.int32, (blk, blk), 1)
    s = jnp.where(row >= col, scores(off), NEG_BIG)
    m, l, acc = update(carry, s, off)
    o_ref[0] = (acc / l).astype(o_ref.dtype)


def _attention(q, k, v, blk=512):
    B, H, S, _ = q.shape
    blk = min(blk, S)
    assert S % blk == 0
    need = 2 * (_nbytes((blk, 256), BF16) + _nbytes((S, 256), BF16) + _nbytes((S, 128), BF16)
                + _nbytes((blk, 128), BF16)) + 6 * _nbytes((blk, blk), F32)
    return pl.pallas_call(
        functools.partial(_attn_kernel, blk=blk),
        grid=(B, H, S // blk),
        in_specs=[pl.BlockSpec((1, 1, blk, MLA_QK_DIM), lambda b, h, i: (b, h, i, 0)),
                  pl.BlockSpec((1, 1, S, MLA_QK_DIM), lambda b, h, i: (b, h, 0, 0)),
                  pl.BlockSpec((1, 1, S, MLA_V_DIM), lambda b, h, i: (b, h, 0, 0))],
        out_specs=pl.BlockSpec((1, blk, MLA_V_DIM), lambda b, h, i: (b, i, h)),
        out_shape=jax.ShapeDtypeStruct((B, S, H * MLA_V_DIM), BF16),
        compiler_params=pltpu.CompilerParams(
            dimension_semantics=("parallel", "parallel", "parallel"),
            vmem_limit_bytes=_vmem_limit(need + (8 << 20))),
        name="mla_attention",
    )(q, k, v)


def _gdn_kernel(pq_ref, pk_ref, pv_ref, pz_ref, hq_ref, hk_ref, hv_ref, cq_ref, ck_ref, cv_ref,
                g_ref, b_ref, ng_ref, o_ref, state_ref, *, n_chunks, k_heads):
    C = CHUNK
    rep = GDN_V_HEADS // GDN_K_HEADS
    v_heads = k_heads * rep
    DK, DV = GDN_K_DIM, GDN_V_DIM
    first_block = pl.program_id(2) == 0

    @pl.when(first_block)
    def _():
        state_ref[...] = jnp.zeros_like(state_ref)

    row = lax.broadcasted_iota(jnp.int32, (C, C), 0)
    col = lax.broadcasted_iota(jnp.int32, (C, C), 1)
    causal = row >= col
    strict = row > col
    eye = row == col
    eye_f = eye.astype(F32)
    nt = (((1,), (1,)), ((), ()))
    tn = (((0,), (0,)), ((), ()))

    def dot(a, b):
        return jnp.dot(a, b, preferred_element_type=F32)

    def conv_silu(x_ref, halo_ref, w_ref, c, r0):
        cur = x_ref[0, pl.ds(r0, C), :]
        prev_start = pl.multiple_of(jnp.maximum(c * (C // 8) - 1, 0) * 8, 8)
        prev_rows = x_ref[0, pl.ds(prev_start, 8), :]
        halo = jnp.where(first_block, 0.0, halo_ref[0])
        ext = jnp.concatenate([jnp.where(c == 0, halo, prev_rows), cur], axis=0)
        acc = cur * w_ref[CONV_K - 1:CONV_K, :]
        for j in range(1, CONV_K):
            acc = acc + pltpu.roll(ext, j, axis=0)[8:] * w_ref[CONV_K - 1 - j:CONV_K - j, :]
        return acc * jax.nn.sigmoid(acc)

    def l2n(t):
        return t * lax.rsqrt(jnp.sum(t * t, axis=-1, keepdims=True) + RMS_EPS)

    def chunk(c, carry):
        r0 = pl.multiple_of(c * C, C)
        qa = conv_silu(pq_ref, hq_ref, cq_ref, c, r0)
        ka = conv_silu(pk_ref, hk_ref, ck_ref, c, r0)
        va = conv_silu(pv_ref, hv_ref, cv_ref, c, r0)
        q = [l2n(qa[:, i * DK:(i + 1) * DK]) * (DK ** -0.5) for i in range(k_heads)]
        k = [l2n(ka[:, i * DK:(i + 1) * DK]) for i in range(k_heads)]
        v = [va[:, h * DV:(h + 1) * DV] for h in range(v_heads)]
        k16 = [t.astype(BF16) for t in k]
        qk16 = [jnp.concatenate([q[i], k[i]], axis=0).astype(BF16) for i in range(k_heads)]
        qkk = [lax.dot_general(qk16[i], k16[i], nt, preferred_element_type=F32) for i in range(k_heads)]

        g_row = [g_ref[0, h, pl.ds(c, 1), :] for h in range(v_heads)]
        b_row = [b_ref[0, h, pl.ds(c, 1), :] for h in range(v_heads)]
        g_col = [jnp.sum(jnp.where(eye, t, 0.0), axis=1, keepdims=True) for t in g_row]
        b_col = [jnp.sum(jnp.where(eye, t, 0.0), axis=1, keepdims=True) for t in b_row]
        g_last = [t[:, C - 1:C] for t in g_row]
        decay = [jnp.exp(jnp.where(causal, g_col[h] - g_row[h], NEG_BIG)) for h in range(v_heads)]
        a = [jnp.where(strict, qkk[h // rep][C:] * b_col[h] * decay[h], 0.0) for h in range(v_heads)]
        qkd = [(qkk[h // rep][:C] * decay[h]).astype(BF16) for h in range(v_heads)]

        a16 = [t.astype(BF16) for t in a]
        x = [dot(t, t) for t in a16]
        p = [eye_f - t for t in a]
        for _ in range(int(math.log2(C)) - 2):
            x16 = [t.astype(BF16) for t in x]
            px = [dot(jnp.concatenate([p[h].astype(BF16), x16[h]], axis=0), x16[h]) for h in range(v_heads)]
            p = [p[h] + px[h][:C] for h in range(v_heads)]
            x = [px[h][C:] for h in range(v_heads)]
        p = [p[h] + dot(p[h].astype(BF16), x[h].astype(BF16)) for h in range(v_heads)]

        eg = [jnp.exp(t) for t in g_col]
        rhs = [jnp.concatenate([v[h] * b_col[h], k[h // rep] * (b_col[h] * eg[h])], axis=1).astype(BF16)
               for h in range(v_heads)]
        sol = [dot(p[h].astype(BF16), rhs[h]) for h in range(v_heads)]
        s = [state_ref[h] for h in range(v_heads)]
        wq = [jnp.concatenate([sol[h][:, DV:], q[h // rep] * eg[h]], axis=0).astype(BF16)
              for h in range(v_heads)]
        ws = [dot(wq[h], s[h].astype(BF16)) for h in range(v_heads)]
        u16 = [(sol[h][:, :DV] - ws[h][:C]).astype(BF16) for h in range(v_heads)]
        k_dec = [(k[h // rep] * jnp.exp(g_last[h] - g_col[h])).astype(BF16) for h in range(v_heads)]
        ds = [lax.dot_general(k_dec[h], u16[h], tn, preferred_element_type=F32) for h in range(v_heads)]
        o = [ws[h][C:] + dot(qkd[h], u16[h]) for h in range(v_heads)]
        for h in range(v_heads):
            state_ref[h] = s[h] * jnp.exp(g_last[h]) + ds[h]
            z = pz_ref[0, pl.ds(r0, C), h * DV:(h + 1) * DV]
            on = o[h] * lax.rsqrt(jnp.mean(o[h] * o[h], axis=-1, keepdims=True) + RMS_EPS) * ng_ref[...]
            o_ref[0, pl.ds(r0, C), h * DV:(h + 1) * DV] = (on * (z * jax.nn.sigmoid(z))).astype(o_ref.dtype)
        return carry

    lax.fori_loop(0, n_chunks, chunk, 0)


def _gated_delta_rule(p, conv_w, g_cum, beta, norm_g, n_chunks=8, k_heads=8):
    B, S, _ = p.shape
    n_chunks = min(n_chunks, S // CHUNK)
    sb = n_chunks * CHUNK
    rep = GDN_V_HEADS // GDN_K_HEADS
    v_heads = k_heads * rep
    wq, wv = k_heads * GDN_K_DIM, v_heads * GDN_V_DIM
    assert S % sb == 0 and GDN_K_HEADS % k_heads == 0
    k_off, v_off, z_off = GDN_QK_WIDTH // wq, 2 * GDN_QK_WIDTH // wv, GDN_CONV_DIM // wv
    hb = sb // 8

    def main(width, off):
        return pl.BlockSpec((1, sb, width), lambda b, h, s: (b, s, off + h))

    def halo(width, off):
        return pl.BlockSpec((1, 8, width), lambda b, h, s: (b, jnp.maximum(s * hb - 1, 0), off + h))

    def cw(width, off):
        return pl.BlockSpec((CONV_K, width), lambda b, h, s: (0, off + h))

    blocks = 2 * sb * (2 * wq + 2 * wv) * 4 + 2 * sb * wv * 2
    return pl.pallas_call(
        functools.partial(_gdn_kernel, n_chunks=n_chunks, k_heads=k_heads),
        grid=(B, GDN_K_HEADS // k_heads, S // sb),
        in_specs=[main(wq, 0), main(wq, k_off), main(wv, v_off), main(wv, z_off),
                  halo(wq, 0), halo(wq, k_off), halo(wv, v_off),
                  cw(wq, 0), cw(wq, k_off), cw(wv, v_off),
                  pl.BlockSpec((1, v_heads, n_chunks, CHUNK), lambda b, h, s: (b, h, s, 0)),
                  pl.BlockSpec((1, v_heads, n_chunks, CHUNK), lambda b, h, s: (b, h, s, 0)),
                  pl.BlockSpec((1, GDN_V_DIM), lambda b, h, s: (0, 0))],
        out_specs=pl.BlockSpec((1, sb, wv), lambda b, h, s: (b, s, h)),
        out_shape=jax.ShapeDtypeStruct((B, S, GDN_V_WIDTH), BF16),
        scratch_shapes=[pltpu.VMEM((v_heads, GDN_K_DIM, GDN_V_DIM), F32)],
        compiler_params=pltpu.CompilerParams(
            dimension_semantics=("parallel", "parallel", "arbitrary"),
            vmem_limit_bytes=_vmem_limit(blocks + (12 << 20))),
        name="gated_delta_rule",
    )(p, p, p, p, p, p, p, conv_w, conv_w, conv_w, g_cum, beta, norm_g)


def _expert_kernel(be_ref, nu_ref, x_ref, wgu_ref, wd_ref, o_ref, wgu16_ref, wd16_ref):
    i = pl.program_id(0)
    used = i < nu_ref[0]
    new_expert = jnp.logical_or(i == 0, be_ref[i] != be_ref[jnp.maximum(i - 1, 0)])

    @pl.when(jnp.logical_and(used, new_expert))
    def _():
        wgu16_ref[...] = wgu_ref[0].astype(BF16)
        wd16_ref[...] = wd_ref[0].astype(BF16)

    @pl.when(used)
    def _():
        gu = jnp.dot(x_ref[...], wgu16_ref[...], preferred_element_type=F32)
        f = gu.shape[-1] // 2
        gate = gu[:, :f]
        act = gate * jax.nn.sigmoid(gate) * gu[:, f:]
        o_ref[...] = jnp.dot(act.astype(BF16), wd16_ref[...], preferred_element_type=F32).astype(o_ref.dtype)

    @pl.when(jnp.logical_not(used))
    def _():
        o_ref[...] = jnp.zeros_like(o_ref)


def _grouped_swiglu(x_rows, w_gate_up, w_down, blk_expert, n_used, tm, out_dtype, name):
    R, D = x_rows.shape
    F2 = w_gate_up.shape[-1]
    tm = min(tm, R)
    assert R % tm == 0
    w_elems = D * F2 + (F2 // 2) * D
    need = (2 * w_elems * 4 + w_elems * 2 + 2 * _nbytes((tm, D), BF16) + 2 * _nbytes((tm, D), out_dtype)
            + 2 * _nbytes((tm, F2), F32) + _nbytes((tm, D), F32))
    grid_spec = pltpu.PrefetchScalarGridSpec(
        num_scalar_prefetch=2,
        grid=(R // tm,),
        in_specs=[pl.BlockSpec((tm, D), lambda i, be, nu: (i, 0)),
                  pl.BlockSpec((1, D, F2), lambda i, be, nu: (be[i], 0, 0)),
                  pl.BlockSpec((1, F2 // 2, D), lambda i, be, nu: (be[i], 0, 0))],
        out_specs=pl.BlockSpec((tm, D), lambda i, be, nu: (i, 0)),
        scratch_shapes=[pltpu.VMEM((D, F2), BF16), pltpu.VMEM((F2 // 2, D), BF16)],
    )
    return pl.pallas_call(
        _expert_kernel,
        grid_spec=grid_spec,
        out_shape=jax.ShapeDtypeStruct((R, D), out_dtype),
        compiler_params=pltpu.CompilerParams(
            dimension_semantics=("arbitrary",),
            vmem_limit_bytes=_vmem_limit(need + (4 << 20))),
        name=name,
    )(blk_expert, n_used, x_rows, w_gate_up, w_down)


def _combine_kernel(x_ref, sh_ref, y_ref, gate_ref, g_ref, b_ref, o_ref, o16_ref):
    D = x_ref.shape[-1]
    acc = ALPHA * x_ref[...] + sh_ref[...]
    gate = gate_ref[...]
    for k in range(TOP_K):
        acc = acc + gate[:, k:k + 1] * y_ref[:, k * D:(k + 1) * D].astype(F32)
    out = _layer_norm_rows(acc, g_ref[...], b_ref[...])
    o_ref[...] = out
    o16_ref[...] = out.astype(BF16)


def _moe_combine(x, shared, y_tok, gate, ln_g, ln_b, tm=256):
    T, D = x.shape
    tm = min(tm, T)
    assert T % tm == 0
    need = 2 * (3 * _nbytes((tm, D), F32) + _nbytes((tm, D), BF16) + _nbytes((tm, TOP_K * D), BF16)) \
        + 3 * _nbytes((tm, D), F32)
    return pl.pallas_call(
        _combine_kernel,
        grid=(T // tm,),
        in_specs=[pl.BlockSpec((tm, D), lambda i: (i, 0)),
                  pl.BlockSpec((tm, D), lambda i: (i, 0)),
                  pl.BlockSpec((tm, TOP_K * D), lambda i: (i, 0)),
                  pl.BlockSpec((tm, TOP_K), lambda i: (i, 0)),
                  pl.BlockSpec((1, D), lambda i: (0, 0)),
                  pl.BlockSpec((1, D), lambda i: (0, 0))],
        out_specs=[pl.BlockSpec((tm, D), lambda i: (i, 0)),
                   pl.BlockSpec((tm, D), lambda i: (i, 0))],
        out_shape=[jax.ShapeDtypeStruct((T, D), F32), jax.ShapeDtypeStruct((T, D), BF16)],
        compiler_params=pltpu.CompilerParams(
            dimension_semantics=("parallel",),
            vmem_limit_bytes=_vmem_limit(need + (4 << 20))),
        name="moe_combine",
    )(x, shared, y_tok, gate, ln_g.reshape(1, D), ln_b.reshape(1, D))


def _rms_norm(x, g):
    return x * lax.rsqrt(jnp.mean(x * x, axis=-1, keepdims=True) + RMS_EPS) * g


def _rope(x, cos, sin):
    half = x.shape[-1] // 2
    x1, x2 = x[..., :half], x[..., half:]
    return jnp.concatenate([x1 * cos - x2 * sin, x2 * cos + x1 * sin], axis=-1)


def _mla_mixer(x, x16, positions, w_in, q_norm_g, kv_norm_g, w_qb, w_kvb, w_o, ln_g, ln_b):
    B, S = positions.shape
    T, D = x.shape
    H = MLA_HEADS
    c = _matmul(x16, w_in.astype(BF16), F32, tn=w_in.shape[1], name="mla_in")
    cq = _rms_norm(c[:, :MLA_Q_RANK], q_norm_g)
    ckv = _rms_norm(c[:, MLA_Q_RANK:MLA_Q_RANK + MLA_KV_RANK], kv_norm_g)
    k_rope = c[:, MLA_Q_RANK + MLA_KV_RANK:].reshape(B, S, MLA_ROPE_DIM)
    q = _matmul(cq.astype(BF16), w_qb.astype(BF16), F32, name="mla_qb").reshape(B, S, H, MLA_QK_DIM)
    kv = _matmul(ckv.astype(BF16), w_kvb.astype(BF16), BF16, name="mla_kvb").reshape(
        B, S, H, MLA_NOPE_DIM + MLA_V_DIM)
    inv_freq = 1.0 / (ROPE_THETA ** (jnp.arange(0, MLA_ROPE_DIM, 2, dtype=F32) / MLA_ROPE_DIM))
    ang = positions.astype(F32)[..., None] * inv_freq
    cos, sin = jnp.cos(ang), jnp.sin(ang)
    q_rope = _rope(q[..., MLA_NOPE_DIM:], cos[:, :, None, :], sin[:, :, None, :])
    k_rope = _rope(k_rope, cos, sin)
    q_cat = jnp.concatenate([q[..., :MLA_NOPE_DIM], q_rope], axis=-1) * MLA_SCALE
    q_cat = jnp.transpose(q_cat, (0, 2, 1, 3)).astype(BF16)
    k_cat = jnp.concatenate(
        [kv[..., :MLA_NOPE_DIM],
         jnp.broadcast_to(k_rope[:, :, None, :].astype(BF16), (B, S, H, MLA_ROPE_DIM))], axis=-1)
    k_cat = jnp.transpose(k_cat, (0, 2, 1, 3))
    v = jnp.transpose(kv[..., MLA_NOPE_DIM:], (0, 2, 1, 3))
    o = _attention(q_cat, k_cat, v)
    return _matmul_ln(o.reshape(T, H * MLA_V_DIM), w_o.astype(BF16), x, ln_g, ln_b, 512, "mla_out_ln")


def _gdn_mixer(x, x16, batch, w_in, conv_w, a_log, dt_bias, norm_g, w_out, ln_g, ln_b):
    T, D = x.shape
    B, S = batch, T // batch
    n_main = GDN_CONV_DIM + GDN_V_WIDTH
    p = _matmul(x16, w_in[:, :n_main].astype(BF16), F32, name="gdn_in")
    ba = _matmul(x16, w_in[:, n_main:].astype(BF16), F32, name="gdn_in_ba")
    b = ba[:, :GDN_V_HEADS].reshape(B, S, GDN_V_HEADS)
    a = ba[:, GDN_V_HEADS:].reshape(B, S, GDN_V_HEADS)
    beta = jax.nn.sigmoid(b)
    g = -jnp.exp(a_log) * jax.nn.softplus(a + dt_bias)
    N = S // CHUNK
    g_cum = jnp.cumsum(g.reshape(B, N, CHUNK, GDN_V_HEADS), axis=2)
    g_cum = jnp.transpose(g_cum, (0, 3, 1, 2))
    beta = jnp.transpose(beta.reshape(B, N, CHUNK, GDN_V_HEADS), (0, 3, 1, 2))
    o = _gated_delta_rule(p.reshape(B, S, n_main), conv_w, g_cum, beta, norm_g.reshape(1, GDN_V_DIM))
    return _matmul_ln(o.reshape(T, GDN_V_WIDTH), w_out.astype(BF16), x, ln_g, ln_b, 256, "gdn_out_ln")


def _moe_ffn(x, x16, router_w, router_bias, w_gate_up, w_down, shared_gate_up, shared_down, ln_g, ln_b):
    T, D = x.shape
    logits = jnp.dot(x, router_w, precision=lax.Precision.HIGHEST)
    scores = jax.nn.sigmoid(logits)
    biased = scores + router_bias
    grp = biased.reshape(T, N_GROUPS, N_EXPERTS // N_GROUPS)
    grp_score = lax.top_k(grp, 2)[0].sum(-1)
    _, top_groups = lax.top_k(grp_score, TOPK_GROUPS)
    group_mask = jnp.any(top_groups[:, :, None] == jnp.arange(N_GROUPS)[None, None, :], axis=1)
    expert_mask = jnp.repeat(group_mask, N_EXPERTS // N_GROUPS, axis=1)
    _, top_e = lax.top_k(jnp.where(expert_mask, biased, -jnp.inf), TOP_K)
    gate = jnp.take_along_axis(scores, top_e, axis=1)
    gate = gate / (gate.sum(-1, keepdims=True) + 1e-20) * ROUTED_SCALE

    zero = jnp.zeros((1,), jnp.int32)
    tm_sh = min(512, T)
    shared = _grouped_swiglu(x16, shared_gate_up[None], shared_down[None],
                             jnp.zeros((T // tm_sh,), jnp.int32), zero + T // tm_sh, tm_sh, F32,
                             "shared_expert")

    n_assign = T * TOP_K
    n_blocks = -(-n_assign // MOE_BLOCK) + N_EXPERTS
    n_rows = n_blocks * MOE_BLOCK
    e_flat = top_e.reshape(n_assign).astype(jnp.int32)
    order = jnp.sort(e_flat * n_assign + jnp.arange(n_assign, dtype=jnp.int32)) % n_assign
    onehot = (top_e[:, :, None] == jnp.arange(N_EXPERTS, dtype=top_e.dtype)).any(axis=1).astype(jnp.int32)
    counts = onehot.sum(axis=0)
    rank = jnp.take_along_axis(jnp.cumsum(onehot, axis=0) - onehot, top_e, axis=1)
    starts = jnp.cumsum(counts) - counts
    padded = (counts + MOE_BLOCK - 1) // MOE_BLOCK * MOE_BLOCK
    pad_ends = jnp.cumsum(padded)
    pad_starts = pad_ends - padded
    pos = (pad_starts[top_e] + rank).reshape(n_assign)
    blk_expert = jnp.minimum(
        jnp.searchsorted(pad_ends, jnp.arange(n_blocks, dtype=jnp.int32) * MOE_BLOCK, side='right'),
        N_EXPERTS - 1).astype(jnp.int32)
    n_used = (pad_ends[-1:] // MOE_BLOCK).astype(jnp.int32)
    row_e = jnp.repeat(blk_expert, MOE_BLOCK)
    row_j = jnp.arange(n_rows, dtype=jnp.int32) - pad_starts[row_e]
    row_src = jnp.clip(starts[row_e] + row_j, 0, n_assign - 1)
    row_tok = jnp.where(row_j < counts[row_e], order[row_src] // TOP_K, 0)

    x_rows = jnp.take(x16, row_tok, axis=0)
    y_rows = _grouped_swiglu(x_rows, w_gate_up, w_down, blk_expert, n_used, MOE_BLOCK, BF16,
                             "routed_experts")
    y_tok = jnp.take(y_rows, pos, axis=0).reshape(T, TOP_K * D)
    return _moe_combine(x, shared, y_tok, gate, ln_g, ln_b)


def kernel(x, positions, mla_w_in, mla_q_norm, mla_kv_norm, mla_w_qb, mla_w_kvb, mla_w_o, gdn_w_in, gdn_conv_w, gdn_a_log, gdn_dt_bias, gdn_norm, gdn_w_out, ln_mix_g, ln_mix_b, ln_ffn_g, ln_ffn_b, moe_router, moe_router_bias, moe_w_gate_up, moe_w_down, moe_shared_gate_up, moe_shared_down):
    B, S, D = x.shape
    x = x.reshape(B * S, D)
    x16 = x.astype(BF16)
    for i in range(DEPTH):
        j = i // 2
        if i % 2 == 0:
            x, x16 = _mla_mixer(x, x16, positions, mla_w_in[j], mla_q_norm[j], mla_kv_norm[j],
                                mla_w_qb[j], mla_w_kvb[j], mla_w_o[j], ln_mix_g[i], ln_mix_b[i])
        else:
            x, x16 = _gdn_mixer(x, x16, B, gdn_w_in[j], gdn_conv_w[j], gdn_a_log[j], gdn_dt_bias[j],
                                gdn_norm[j], gdn_w_out[j], ln_mix_g[i], ln_mix_b[i])
        x, x16 = _moe_ffn(x, x16, moe_router[i], moe_router_bias[i], moe_w_gate_up[i], moe_w_down[i],
                          moe_shared_gate_up[i], moe_shared_down[i], ln_ffn_g[i], ln_ffn_b[i])
    return x.reshape(B, S, D)
```

```python
import functools
import math

import jax
import jax.numpy as jnp
from jax import lax
from jax.experimental import pallas as pl
from jax.experimental.pallas import tpu as pltpu

F32 = jnp.float32
BF16 = jnp.bfloat16

DEPTH = 2
MLA_HEADS = 16
MLA_Q_RANK = 512
MLA_KV_RANK = 512
MLA_NOPE_DIM = 128
MLA_ROPE_DIM = 64
MLA_V_DIM = 128
MLA_QK_DIM = MLA_NOPE_DIM + MLA_ROPE_DIM
MLA_SCALE = MLA_QK_DIM ** -0.5
ROPE_THETA = 10000.0

GDN_K_HEADS = 16
GDN_V_HEADS = 32
GDN_K_DIM = 128
GDN_V_DIM = 128
CONV_K = 4
CHUNK = 64
GDN_QK_WIDTH = GDN_K_HEADS * GDN_K_DIM
GDN_V_WIDTH = GDN_V_HEADS * GDN_V_DIM
GDN_CONV_DIM = 2 * GDN_QK_WIDTH + GDN_V_WIDTH

N_EXPERTS = 64
TOP_K = 8
N_GROUPS = 8
TOPK_GROUPS = 4
EXPERT_DIM = 512
ROUTED_SCALE = 2.5
MOE_BLOCK = 256

ALPHA = (2 * DEPTH) ** 0.25
LN_EPS = 1e-5
RMS_EPS = 1e-6

V7X_VMEM_BYTES = 64 * 1024 * 1024
VMEM_LIMIT_CAP = 56 * 1024 * 1024
NEG_BIG = -1e30


def _vmem_limit(nbytes):
    return int(min(max(nbytes, 16 * 1024 * 1024), VMEM_LIMIT_CAP))


def _nbytes(shape, dtype):
    return math.prod(shape) * jnp.dtype(dtype).itemsize


def _layer_norm_rows(y, g, b):
    mu = jnp.mean(y, axis=-1, keepdims=True)
    yc = y - mu
    var = jnp.mean(yc * yc, axis=-1, keepdims=True)
    return yc * lax.rsqrt(var + LN_EPS) * g + b


def _mm_kernel(x_ref, w_ref, o_ref):
    o_ref[...] = jnp.dot(x_ref[...], w_ref[...], preferred_element_type=F32).astype(o_ref.dtype)


def _matmul(x, w, out_dtype, tm=1024, tn=1024, name="matmul"):
    M, K = x.shape
    N = w.shape[1]
    tm = min(tm, M)
    tn = min(tn, N)
    assert M % tm == 0 and N % tn == 0, (M, N, tm, tn)
    need = 2 * (_nbytes((tm, K), x.dtype) + _nbytes((K, tn), w.dtype) + _nbytes((tm, tn), out_dtype))
    need += _nbytes((tm, tn), F32)
    return pl.pallas_call(
        _mm_kernel,
        grid=(N // tn, M // tm),
        in_specs=[pl.BlockSpec((tm, K), lambda j, i: (i, 0)),
                  pl.BlockSpec((K, tn), lambda j, i: (0, j))],
        out_specs=pl.BlockSpec((tm, tn), lambda j, i: (i, j)),
        out_shape=jax.ShapeDtypeStruct((M, N), out_dtype),
        compiler_params=pltpu.CompilerParams(
            dimension_semantics=("parallel", "parallel"),
            vmem_limit_bytes=_vmem_limit(need + (4 << 20))),
        name=name,
    )(x, w)


def _mm_ln_kernel(x_ref, w_ref, r_ref, g_ref, b_ref, o_ref, o16_ref):
    y = ALPHA * r_ref[...] + jnp.dot(x_ref[...], w_ref[...], preferred_element_type=F32)
    out = _layer_norm_rows(y, g_ref[...], b_ref[...])
    o_ref[...] = out
    o16_ref[...] = out.astype(BF16)


def _matmul_ln(x, w, resid, ln_g, ln_b, tm, name):
    M, K = x.shape
    N = w.shape[1]
    tm = min(tm, M)
    assert M % tm == 0
    need = (_nbytes((K, N), BF16) + 2 * _nbytes((tm, K), BF16) + 4 * _nbytes((tm, N), F32)
            + 2 * _nbytes((tm, N), BF16) + 2 * _nbytes((tm, N), F32))
    return pl.pallas_call(
        _mm_ln_kernel,
        grid=(M // tm,),
        in_specs=[pl.BlockSpec((tm, K), lambda i: (i, 0)),
                  pl.BlockSpec((K, N), lambda i: (0, 0), pipeline_mode=pl.Buffered(1)),
                  pl.BlockSpec((tm, N), lambda i: (i, 0)),
                  pl.BlockSpec((1, N), lambda i: (0, 0)),
                  pl.BlockSpec((1, N), lambda i: (0, 0))],
        out_specs=[pl.BlockSpec((tm, N), lambda i: (i, 0)),
                   pl.BlockSpec((tm, N), lambda i: (i, 0))],
        out_shape=[jax.ShapeDtypeStruct((M, N), F32), jax.ShapeDtypeStruct((M, N), BF16)],
        compiler_params=pltpu.CompilerParams(
            dimension_semantics=("parallel",),
            vmem_limit_bytes=_vmem_limit(need + (4 << 20))),
        name=name,
    )(x, w, resid, ln_g.reshape(1, N), ln_b.reshape(1, N))


def _attn_kernel(q_ref, k_ref, v_ref, o_ref, *, blk):
    i = pl.program_id(2)
    wide = 2
    q = q_ref[0, 0]

    def step(carry, off, width, mask):
        m, l, acc = carry
        k = k_ref[0, 0, pl.ds(off, width), :]
        s = lax.dot_general(q, k, (((1,), (1,)), ((), ())), preferred_element_type=F32)
        if mask is not None:
            s = jnp.where(mask, s, NEG_BIG)
        m_new = jnp.maximum(m, jnp.max(s, axis=-1, keepdims=True))
        alpha = jnp.exp(m - m_new)
        p = jnp.exp(s - m_new)
        l = alpha * l + jnp.sum(p, axis=-1, keepdims=True)
        v = v_ref[0, 0, pl.ds(off, width), :]
        acc = alpha * acc + jnp.dot(p.astype(BF16), v, preferred_element_type=F32)
        return m_new, l, acc

    carry = (jnp.full((blk, 1), NEG_BIG, F32), jnp.zeros((blk, 1), F32), jnp.zeros((blk, MLA_V_DIM), F32))
    n_wide = i // wide
    carry = lax.fori_loop(
        0, n_wide, lambda j, c: step(c, pl.multiple_of(j * (wide * blk), wide * blk), wide * blk, None), carry)
    carry = lax.fori_loop(
        n_wide * wide, i, lambda j, c: step(c, pl.multiple_of(j * blk, blk), blk, None), carry)
    row = lax.broadcasted_iota(jnp.int32, (blk, blk), 0)
    col = lax.broadcasted_iota(jnp.int32, (blk, blk), 1)
    m, l, acc = step(carry, pl.multiple_of(i * blk, blk), blk, row >= col)
    o_ref[0] = (acc / l).astype(o_ref.dtype)


def _attention(q, k, v, blk=512):
    B, H, S, qk_w = q.shape
    blk = min(blk, S)
    assert S % blk == 0
    need = 2 * (_nbytes((blk, 256), BF16) + _nbytes((S, 256), BF16) + _nbytes((S, 128), BF16)
                + _nbytes((blk, 128), BF16)) + 6 * _nbytes((blk, blk), F32)
    return pl.pallas_call(
        functools.partial(_attn_kernel, blk=blk),
        grid=(B, H, S // blk),
        in_specs=[pl.BlockSpec((1, 1, blk, qk_w), lambda b, h, i: (b, h, i, 0)),
                  pl.BlockSpec((1, 1, S, qk_w), lambda b, h, i: (b, h, 0, 0)),
                  pl.BlockSpec((1, 1, S, MLA_V_DIM), lambda b, h, i: (b, h, 0, 0))],
        out_specs=pl.BlockSpec((1, blk, MLA_V_DIM), lambda b, h, i: (b, i, h)),
        out_shape=jax.ShapeDtypeStruct((B, S, H * MLA_V_DIM), BF16),
        compiler_params=pltpu.CompilerParams(
            dimension_semantics=("parallel", "parallel", "parallel"),
            vmem_limit_bytes=_vmem_limit(need + (8 << 20))),
        name="mla_attention",
    )(q, k, v)


def _gdn_kernel(pq_ref, pk_ref, pv_ref, pz_ref, hq_ref, hk_ref, hv_ref, cq_ref, ck_ref, cv_ref,
                g_ref, b_ref, ng_ref, o_ref, state_ref, *, n_chunks, k_heads):
    C = CHUNK
    rep = GDN_V_HEADS // GDN_K_HEADS
    v_heads = k_heads * rep
    DK, DV = GDN_K_DIM, GDN_V_DIM
    first_block = pl.program_id(2) == 0

    @pl.when(first_block)
    def _():
        state_ref[...] = jnp.zeros_like(state_ref)

    row = lax.broadcasted_iota(jnp.int32, (C, C), 0)
    col = lax.broadcasted_iota(jnp.int32, (C, C), 1)
    causal = row >= col
    strict = row > col
    eye = row == col
    eye_f = eye.astype(F32)
    nt = (((1,), (1,)), ((), ()))
    tn = (((0,), (0,)), ((), ()))

    def dot(a, b):
        return jnp.dot(a, b, preferred_element_type=F32)

    def conv_silu(x_ref, halo_ref, w_ref, c, r0):
        cur = x_ref[0, pl.ds(r0, C), :]
        prev_start = pl.multiple_of(jnp.maximum(c * (C // 8) - 1, 0) * 8, 8)
        prev_rows = x_ref[0, pl.ds(prev_start, 8), :]
        halo = jnp.where(first_block, 0.0, halo_ref[0])
        ext = jnp.concatenate([jnp.where(c == 0, halo, prev_rows), cur], axis=0)
        acc = cur * w_ref[CONV_K - 1:CONV_K, :]
        for j in range(1, CONV_K):
            acc = acc + pltpu.roll(ext, j, axis=0)[8:] * w_ref[CONV_K - 1 - j:CONV_K - j, :]
        return acc * jax.nn.sigmoid(acc)

    def l2n(t):
        return t * lax.rsqrt(jnp.sum(t * t, axis=-1, keepdims=True) + RMS_EPS)

    def chunk(c, carry):
        r0 = pl.multiple_of(c * C, C)
        qa = conv_silu(pq_ref, hq_ref, cq_ref, c, r0)
        ka = conv_silu(pk_ref, hk_ref, ck_ref, c, r0)
        va = conv_silu(pv_ref, hv_ref, cv_ref, c, r0)
        q = [l2n(qa[:, i * DK:(i + 1) * DK]) * (DK ** -0.5) for i in range(k_heads)]
        k = [l2n(ka[:, i * DK:(i + 1) * DK]) for i in range(k_heads)]
        v = [va[:, h * DV:(h + 1) * DV] for h in range(v_heads)]
        k16 = [t.astype(BF16) for t in k]
        qk16 = [jnp.concatenate([q[i], k[i]], axis=0).astype(BF16) for i in range(k_heads)]
        qkk = [lax.dot_general(qk16[i], k16[i], nt, preferred_element_type=F32) for i in range(k_heads)]

        g_row = [g_ref[0, h, pl.ds(c, 1), :] for h in range(v_heads)]
        b_row = [b_ref[0, h, pl.ds(c, 1), :] for h in range(v_heads)]
        g_col = [jnp.sum(jnp.where(eye, t, 0.0), axis=1, keepdims=True) for t in g_row]
        b_col = [jnp.sum(jnp.where(eye, t, 0.0), axis=1, keepdims=True) for t in b_row]
        g_last = [t[:, C - 1:C] for t in g_row]
        decay = [jnp.exp(jnp.where(causal, g_col[h] - g_row[h], NEG_BIG)) for h in range(v_heads)]
        a = [jnp.where(strict, qkk[h // rep][C:] * b_col[h] * decay[h], 0.0) for h in range(v_heads)]
        qkd = [(qkk[h // rep][:C] * decay[h]).astype(BF16) for h in range(v_heads)]

        a16 = [t.astype(BF16) for t in a]
        x = [dot(t, t) for t in a16]
        p = [eye_f - t for t in a]
        for _ in range(int(math.log2(C)) - 2):
            x16 = [t.astype(BF16) for t in x]
            px = [dot(jnp.concatenate([p[h].astype(BF16), x16[h]], axis=0), x16[h]) for h in range(v_heads)]
            p = [p[h] + px[h][:C] for h in range(v_heads)]
            x = [px[h][C:] for h in range(v_heads)]
        p = [p[h] + dot(p[h].astype(BF16), x[h].astype(BF16)) for h in range(v_heads)]

        eg = [jnp.exp(t) for t in g_col]
        rhs = [jnp.concatenate([v[h] * b_col[h], k[h // rep] * (b_col[h] * eg[h])], axis=1).astype(BF16)
               for h in range(v_heads)]
        sol = [dot(p[h].astype(BF16), rhs[h]) for h in range(v_heads)]
        s = [state_ref[h] for h in range(v_heads)]
        wq = [jnp.concatenate([sol[h][:, DV:], q[h // rep] * eg[h]], axis=0).astype(BF16)
              for h in range(v_heads)]
        ws = [dot(wq[h], s[h].astype(BF16)) for h in range(v_heads)]
        u16 = [(sol[h][:, :DV] - ws[h][:C]).astype(BF16) for h in range(v_heads)]
        k_dec = [(k[h // rep] * jnp.exp(g_last[h] - g_col[h])).astype(BF16) for h in range(v_heads)]
        ds = [lax.dot_general(k_dec[h], u16[h], tn, preferred_element_type=F32) for h in range(v_heads)]
        o = [ws[h][C:] + dot(qkd[h], u16[h]) for h in range(v_heads)]
        for h in range(v_heads):
            state_ref[h] = s[h] * jnp.exp(g_last[h]) + ds[h]
            z = pz_ref[0, pl.ds(r0, C), h * DV:(h + 1) * DV]
            on = o[h] * lax.rsqrt(jnp.mean(o[h] * o[h], axis=-1, keepdims=True) + RMS_EPS) * ng_ref[...]
            o_ref[0, pl.ds(r0, C), h * DV:(h + 1) * DV] = (on * (z * jax.nn.sigmoid(z))).astype(o_ref.dtype)
        return carry

    lax.fori_loop(0, n_chunks, chunk, 0)


def _gated_delta_rule(p, conv_w, g_cum, beta, norm_g, n_chunks=8, k_heads=8):
    B, S, _ = p.shape
    n_chunks = min(n_chunks, S // CHUNK)
    sb = n_chunks * CHUNK
    rep = GDN_V_HEADS // GDN_K_HEADS
    v_heads = k_heads * rep
    wq, wv = k_heads * GDN_K_DIM, v_heads * GDN_V_DIM
    assert S % sb == 0 and GDN_K_HEADS % k_heads == 0
    k_off, v_off, z_off = GDN_QK_WIDTH // wq, 2 * GDN_QK_WIDTH // wv, GDN_CONV_DIM // wv
    hb = sb // 8

    def main(width, off):
        return pl.BlockSpec((1, sb, width), lambda b, h, s: (b, s, off + h))

    def halo(width, off):
        return pl.BlockSpec((1, 8, width), lambda b, h, s: (b, jnp.maximum(s * hb - 1, 0), off + h))

    def cw(width, off):
        return pl.BlockSpec((CONV_K, width), lambda b, h, s: (0, off + h))

    blocks = 2 * sb * (2 * wq + 2 * wv) * 4 + 2 * sb * wv * 2
    return pl.pallas_call(
        functools.partial(_gdn_kernel, n_chunks=n_chunks, k_heads=k_heads),
        grid=(B, GDN_K_HEADS // k_heads, S // sb),
        in_specs=[main(wq, 0), main(wq, k_off), main(wv, v_off), main(wv, z_off),
                  halo(wq, 0), halo(wq, k_off), halo(wv, v_off),
                  cw(wq, 0), cw(wq, k_off), cw(wv, v_off),
                  pl.BlockSpec((1, v_heads, n_chunks, CHUNK), lambda b, h, s: (b, h, s, 0)),
                  pl.BlockSpec((1, v_heads, n_chunks, CHUNK), lambda b, h, s: (b, h, s, 0)),
                  pl.BlockSpec((1, GDN_V_DIM), lambda b, h, s: (0, 0))],
        out_specs=pl.BlockSpec((1, sb, wv), lambda b, h, s: (b, s, h)),
        out_shape=jax.ShapeDtypeStruct((B, S, GDN_V_WIDTH), BF16),
        scratch_shapes=[pltpu.VMEM((v_heads, GDN_K_DIM, GDN_V_DIM), F32)],
        compiler_params=pltpu.CompilerParams(
            dimension_semantics=("parallel", "parallel", "arbitrary"),
            vmem_limit_bytes=_vmem_limit(blocks + (12 << 20))),
        name="gated_delta_rule",
    )(p, p, p, p, p, p, p, conv_w, conv_w, conv_w, g_cum, beta, norm_g)


def _expert_kernel(be_ref, nu_ref, x_ref, wgu_ref, wd_ref, o_ref, wgu16_ref, wd16_ref):
    i = pl.program_id(0)
    used = i < nu_ref[0]
    new_expert = jnp.logical_or(i == 0, be_ref[i] != be_ref[jnp.maximum(i - 1, 0)])

    @pl.when(jnp.logical_and(used, new_expert))
    def _():
        wgu16_ref[...] = wgu_ref[0, 0].astype(BF16)
        wd16_ref[...] = wd_ref[0, 0].astype(BF16)

    @pl.when(used)
    def _():
        gu = jnp.dot(x_ref[...], wgu16_ref[...], preferred_element_type=F32)
        f = gu.shape[-1] // 2
        gate = gu[:, :f]
        act = gate * jax.nn.sigmoid(gate) * gu[:, f:]
        o_ref[...] = jnp.dot(act.astype(BF16), wd16_ref[...], preferred_element_type=F32).astype(o_ref.dtype)

    @pl.when(jnp.logical_not(used))
    def _():
        o_ref[...] = jnp.zeros_like(o_ref)


def _grouped_swiglu(x_rows, w_gate_up, w_down, layer, blk_expert, n_used, tm, out_dtype, name):
    R, D = x_rows.shape
    F2 = w_gate_up.shape[-1]
    tm = min(tm, R)
    assert R % tm == 0
    w_elems = D * F2 + (F2 // 2) * D
    need = (2 * w_elems * 4 + w_elems * 2 + 2 * _nbytes((tm, D), BF16) + 2 * _nbytes((tm, D), out_dtype)
            + 2 * _nbytes((tm, F2), F32) + _nbytes((tm, D), F32))
    grid_spec = pltpu.PrefetchScalarGridSpec(
        num_scalar_prefetch=2,
        grid=(R // tm,),
        in_specs=[pl.BlockSpec((tm, D), lambda i, be, nu: (i, 0)),
                  pl.BlockSpec((1, 1, D, F2), lambda i, be, nu: (layer, be[i], 0, 0)),
                  pl.BlockSpec((1, 1, F2 // 2, D), lambda i, be, nu: (layer, be[i], 0, 0))],
        out_specs=pl.BlockSpec((tm, D), lambda i, be, nu: (i, 0)),
        scratch_shapes=[pltpu.VMEM((D, F2), BF16), pltpu.VMEM((F2 // 2, D), BF16)],
    )
    return pl.pallas_call(
        _expert_kernel,
        grid_spec=grid_spec,
        out_shape=jax.ShapeDtypeStruct((R, D), out_dtype),
        compiler_params=pltpu.CompilerParams(
            dimension_semantics=("arbitrary",),
            vmem_limit_bytes=_vmem_limit(need + (4 << 20))),
        name=name,
    )(blk_expert, n_used, x_rows, w_gate_up, w_down)


def _combine_kernel(x_ref, sh_ref, y_ref, gate_ref, g_ref, b_ref, o_ref, o16_ref):
    acc = ALPHA * x_ref[...] + sh_ref[...]
    gate = gate_ref[...]
    for k in range(TOP_K):
        acc = acc + gate[:, k:k + 1] * y_ref[k].astype(F32)
    out = _layer_norm_rows(acc, g_ref[...], b_ref[...])
    o_ref[...] = out
    o16_ref[...] = out.astype(BF16)


def _moe_combine(x, shared, y_tok, gate, ln_g, ln_b, tm=256):
    T, D = x.shape
    tm = min(tm, T)
    assert T % tm == 0
    need = 2 * (3 * _nbytes((tm, D), F32) + _nbytes((tm, D), BF16) + _nbytes((tm, TOP_K * D), BF16)) \
        + 3 * _nbytes((tm, D), F32)
    return pl.pallas_call(
        _combine_kernel,
        grid=(T // tm,),
        in_specs=[pl.BlockSpec((tm, D), lambda i: (i, 0)),
                  pl.BlockSpec((tm, D), lambda i: (i, 0)),
                  pl.BlockSpec((TOP_K, tm, D), lambda i: (0, i, 0)),
                  pl.BlockSpec((tm, TOP_K), lambda i: (i, 0)),
                  pl.BlockSpec((1, D), lambda i: (0, 0)),
                  pl.BlockSpec((1, D), lambda i: (0, 0))],
        out_specs=[pl.BlockSpec((tm, D), lambda i: (i, 0)),
                   pl.BlockSpec((tm, D), lambda i: (i, 0))],
        out_shape=[jax.ShapeDtypeStruct((T, D), F32), jax.ShapeDtypeStruct((T, D), BF16)],
        compiler_params=pltpu.CompilerParams(
            dimension_semantics=("parallel",),
            vmem_limit_bytes=_vmem_limit(need + (4 << 20))),
        name="moe_combine",
    )(x, shared, y_tok, gate, ln_g.reshape(1, D), ln_b.reshape(1, D))


MLA_QK_PAD = 2 * MLA_NOPE_DIM
MLA_IN_PAD = MLA_Q_RANK + MLA_KV_RANK + 2 * MLA_NOPE_DIM


def _mla_proj_kernel(x_ref, win_ref, wq_ref, wkv_ref, qg_ref, kvg_ref, csq_ref, ck_ref, sk_ref,
                     q_ref, k_ref, v_ref):
    NP = MLA_NOPE_DIM
    c = jnp.dot(x_ref[...], win_ref[...], preferred_element_type=F32)

    def rms(t, g):
        return (t * lax.rsqrt(jnp.mean(t * t, axis=-1, keepdims=True) + RMS_EPS) * g).astype(BF16)

    cq = rms(c[:, :MLA_Q_RANK], qg_ref[...])
    ckv = rms(c[:, MLA_Q_RANK:MLA_Q_RANK + MLA_KV_RANK], kvg_ref[...])
    o = MLA_Q_RANK + MLA_KV_RANK
    k_rope = (c[:, o:o + NP] * ck_ref[...] + c[:, o + NP:o + 2 * NP] * sk_ref[...]).astype(BF16)
    low = lax.broadcasted_iota(jnp.int32, (x_ref.shape[0], NP), 1) < MLA_ROPE_DIM
    csq = csq_ref[...]
    for h in range(MLA_HEADS):
        qh = jnp.dot(cq, wq_ref[:, h * 2 * NP:(h + 1) * 2 * NP], preferred_element_type=F32)
        u = qh[:, NP:] * csq
        q_rope = jnp.where(low, u + pltpu.roll(u, MLA_ROPE_DIM, axis=1), 0.0)
        q_ref[0, h, :, :NP] = (qh[:, :NP] * MLA_SCALE).astype(BF16)
        q_ref[0, h, :, NP:] = (q_rope * MLA_SCALE).astype(BF16)
        kvh = jnp.dot(ckv, wkv_ref[:, h * 2 * NP:(h + 1) * 2 * NP], preferred_element_type=F32)
        k_ref[0, h, :, :NP] = kvh[:, :NP].astype(BF16)
        k_ref[0, h, :, NP:] = k_rope
        v_ref[0, h] = kvh[:, NP:].astype(BF16)


def _mla_proj(x16, w_in_ext, w_q_ext, w_kvb, q_norm_g, kv_norm_g, csq, ck, sk, batch, tm=256):
    T, D = x16.shape
    S = T // batch
    H = MLA_HEADS
    tm = min(tm, S)
    assert S % tm == 0
    nsb = S // tm
    w_bytes = (w_in_ext.size + w_q_ext.size + w_kvb.size) * 2
    out_bytes = _nbytes((H, tm, 2 * MLA_QK_PAD + MLA_V_DIM), BF16)
    need = w_bytes + 2 * _nbytes((tm, D), BF16) + 2 * out_bytes + 4 * _nbytes((tm, MLA_IN_PAD), F32)

    def const(shape):
        return pl.BlockSpec(shape, lambda i: (0, 0), pipeline_mode=pl.Buffered(1))

    def rows(width):
        return pl.BlockSpec((tm, width), lambda i: (i, 0))

    def heads(width):
        return pl.BlockSpec((1, H, tm, width), lambda i: (i // nsb, 0, i % nsb, 0))

    return pl.pallas_call(
        _mla_proj_kernel,
        grid=(T // tm,),
        in_specs=[rows(D), const(w_in_ext.shape), const(w_q_ext.shape), const(w_kvb.shape),
                  const((1, MLA_Q_RANK)), const((1, MLA_KV_RANK)),
                  rows(MLA_NOPE_DIM), rows(MLA_NOPE_DIM), rows(MLA_NOPE_DIM)],
        out_specs=[heads(MLA_QK_PAD), heads(MLA_QK_PAD), heads(MLA_V_DIM)],
        out_shape=[jax.ShapeDtypeStruct((batch, H, S, MLA_QK_PAD), BF16),
                   jax.ShapeDtypeStruct((batch, H, S, MLA_QK_PAD), BF16),
                   jax.ShapeDtypeStruct((batch, H, S, MLA_V_DIM), BF16)],
        compiler_params=pltpu.CompilerParams(
            dimension_semantics=("parallel",),
            vmem_limit_bytes=_vmem_limit(need + (6 << 20))),
        name="mla_proj",
    )(x16, w_in_ext, w_q_ext, w_kvb, q_norm_g.reshape(1, -1), kv_norm_g.reshape(1, -1), csq, ck, sk)


def _router_kernel(x_ref, wt_ref, bias_ref, e_ref, gate_ref, rank_ref, cnt_ref, carry_ref):
    E, G, K = N_EXPERTS, N_GROUPS, TOP_K
    P = E // G
    tm = x_ref.shape[0]

    @pl.when(pl.program_id(0) == 0)
    def _():
        carry_ref[...] = jnp.zeros_like(carry_ref)

    logits = lax.dot_general(wt_ref[...], x_ref[...], (((1,), (1,)), ((), ())),
                             precision=lax.Precision.HIGHEST, preferred_element_type=F32)
    scores = jax.nn.sigmoid(logits)
    biased = scores + bias_ref[...]

    def preceding(v):
        n = v.shape[0]
        idx = lax.broadcasted_iota(jnp.int32, (n, 1), 0)
        cnt = jnp.zeros(v.shape, F32)
        for j in range(n):
            row = v[j:j + 1, :]
            tie = jnp.where(idx > j, 1.0, 0.0)
            cnt = cnt + jnp.where(row > v, 1.0, jnp.where(row == v, tie, 0.0))
        return cnt

    blocks = [biased[g * P:(g + 1) * P, :] for g in range(G)]
    group_score = jnp.concatenate(
        [jnp.sum(jnp.where(preceding(b) < 2.0, b, 0.0), axis=0, keepdims=True) for b in blocks], axis=0)
    group_sel = preceding(group_score) < float(TOPK_GROUPS)
    masked = jnp.concatenate(
        [jnp.where(group_sel[g:g + 1, :], blocks[g], -jnp.inf) for g in range(G)], axis=0)
    sel = preceding(masked) < float(K)
    sel_f = jnp.where(sel, 1.0, 0.0)
    gate = jnp.where(sel, scores, 0.0)
    gate = gate / (jnp.sum(gate, axis=0, keepdims=True) + 1e-20) * ROUTED_SCALE

    t_row = lax.broadcasted_iota(jnp.int32, (tm, tm), 0)
    t_col = lax.broadcasted_iota(jnp.int32, (tm, tm), 1)
    before_t = jnp.where(t_row < t_col, 1.0, 0.0).astype(BF16)
    sel16 = sel_f.astype(BF16)
    rank = jnp.dot(sel16, before_t, preferred_element_type=F32) + carry_ref[...]
    carry_ref[...] = carry_ref[...] + jnp.sum(sel_f, axis=1, keepdims=True)
    e_row = lax.broadcasted_iota(jnp.int32, (E, E), 0)
    e_col = lax.broadcasted_iota(jnp.int32, (E, E), 1)
    before_e = jnp.where(e_col < e_row, 1.0, 0.0).astype(BF16)
    slot = jnp.where(sel, jnp.dot(before_e, sel16, preferred_element_type=F32), -1.0)
    e_idx = lax.broadcasted_iota(jnp.int32, (E, 1), 0).astype(F32)

    def pick(k, t):
        return jnp.sum(jnp.where(slot == float(k), t, 0.0), axis=0, keepdims=True)

    e_ref[...] = jnp.concatenate([pick(k, e_idx) for k in range(K)], axis=0).astype(jnp.int32)
    rank_ref[...] = jnp.concatenate([pick(k, rank) for k in range(K)], axis=0).astype(jnp.int32)
    gate_ref[...] = jnp.concatenate([pick(k, gate) for k in range(K)], axis=0)
    cnt_ref[...] = jnp.broadcast_to(carry_ref[...], cnt_ref.shape).astype(jnp.int32)


def _router(x, router_wt, router_bias, tm=512):
    T, D = x.shape
    E = router_wt.shape[0]
    tm = min(tm, T)
    assert T % tm == 0
    kt = pl.BlockSpec((TOP_K, tm), lambda i: (0, i))
    top_e, gate, rank, cnt = pl.pallas_call(
        _router_kernel,
        grid=(T // tm,),
        in_specs=[pl.BlockSpec((tm, D), lambda i: (i, 0)),
                  pl.BlockSpec((E, D), lambda i: (0, 0)),
                  pl.BlockSpec((E, 1), lambda i: (0, 0))],
        out_specs=[kt, kt, kt, pl.BlockSpec((E, 128), lambda i: (0, 0))],
        out_shape=[jax.ShapeDtypeStruct((TOP_K, T), jnp.int32), jax.ShapeDtypeStruct((TOP_K, T), F32),
                   jax.ShapeDtypeStruct((TOP_K, T), jnp.int32), jax.ShapeDtypeStruct((E, 128), jnp.int32)],
        scratch_shapes=[pltpu.VMEM((E, 1), F32)],
        compiler_params=pltpu.CompilerParams(
            dimension_semantics=("arbitrary",),
            vmem_limit_bytes=_vmem_limit(2 * _nbytes((tm, D), F32) + (16 << 20))),
        name="moe_router",
    )(x, router_wt, router_bias.reshape(E, 1))
    return top_e, gate, rank, cnt[:, 0]


def _mla_mixer(x, x16, positions, w_in, q_norm_g, kv_norm_g, w_qb, w_kvb, w_o, ln_g, ln_b):
    B, S = positions.shape
    T, D = x.shape
    H = MLA_HEADS
    half = MLA_ROPE_DIM // 2

    def rot(w):
        return jnp.concatenate([-w[..., half:], w[..., :half]], axis=-1)

    lat = MLA_Q_RANK + MLA_KV_RANK
    w_kr = w_in[:, lat:]
    zpad = jnp.zeros((D, MLA_NOPE_DIM - MLA_ROPE_DIM), w_in.dtype)
    w_in_ext = jnp.concatenate([w_in[:, :lat], w_kr, zpad, rot(w_kr), zpad], axis=1).astype(BF16)
    w_q3 = w_qb.reshape(MLA_Q_RANK, H, MLA_QK_DIM)
    w_q_rope = w_q3[..., MLA_NOPE_DIM:]
    w_q_ext = jnp.concatenate([w_q3[..., :MLA_NOPE_DIM], w_q_rope, rot(w_q_rope)], axis=-1)
    w_q_ext = w_q_ext.reshape(MLA_Q_RANK, H * MLA_QK_PAD).astype(BF16)
    inv_freq = 1.0 / (ROPE_THETA ** (jnp.arange(0, MLA_ROPE_DIM, 2, dtype=F32) / MLA_ROPE_DIM))
    ang = positions.astype(F32).reshape(T, 1) * inv_freq
    cos, sin = jnp.cos(ang), jnp.sin(ang)
    zero = jnp.zeros_like(cos)
    csq = jnp.concatenate([cos, cos, sin, sin], axis=1)
    ck = jnp.concatenate([cos, cos, zero, zero], axis=1)
    sk = jnp.concatenate([sin, sin, zero, zero], axis=1)
    q_cat, k_cat, v = _mla_proj(x16, w_in_ext, w_q_ext, w_kvb.astype(BF16), q_norm_g, kv_norm_g,
                                csq, ck, sk, B)
    o = _attention(q_cat, k_cat, v)
    return _matmul_ln(o.reshape(T, H * MLA_V_DIM), w_o.astype(BF16), x, ln_g, ln_b, 512, "mla_out_ln")


def _gdn_mixer(x, x16, batch, w_in, conv_w, a_log, dt_bias, norm_g, w_out, ln_g, ln_b):
    T, D = x.shape
    B, S = batch, T // batch
    n_main = GDN_CONV_DIM + GDN_V_WIDTH
    p = _matmul(x16, w_in[:, :n_main].astype(BF16), F32, name="gdn_in")
    ba = _matmul(x16, w_in[:, n_main:].astype(BF16), F32, name="gdn_in_ba")
    b = ba[:, :GDN_V_HEADS].reshape(B, S, GDN_V_HEADS)
    a = ba[:, GDN_V_HEADS:].reshape(B, S, GDN_V_HEADS)
    beta = jax.nn.sigmoid(b)
    g = -jnp.exp(a_log) * jax.nn.softplus(a + dt_bias)
    N = S // CHUNK
    g_cum = jnp.cumsum(g.reshape(B, N, CHUNK, GDN_V_HEADS), axis=2)
    g_cum = jnp.transpose(g_cum, (0, 3, 1, 2))
    beta = jnp.transpose(beta.reshape(B, N, CHUNK, GDN_V_HEADS), (0, 3, 1, 2))
    o = _gated_delta_rule(p.reshape(B, S, n_main), conv_w, g_cum, beta, norm_g.reshape(1, GDN_V_DIM))
    return _matmul_ln(o.reshape(T, GDN_V_WIDTH), w_out.astype(BF16), x, ln_g, ln_b, 256, "gdn_out_ln")


def _moe_ffn(x, x16, layer, router_w, router_bias, w_gate_up, w_down, shared_gate_up, shared_down, ln_g, ln_b):
    T, D = x.shape
    E, K = N_EXPERTS, TOP_K
    top_e, gate, rank, counts = _router(x, router_w.T, router_bias)

    zero = jnp.zeros((1,), jnp.int32)
    tm_sh = min(512, T)
    shared = _grouped_swiglu(x16, shared_gate_up, shared_down, layer,
                             jnp.zeros((T // tm_sh,), jnp.int32), zero + T // tm_sh, tm_sh, F32,
                             "shared_expert")

    n_assign = T * K
    n_blocks = -(-n_assign // MOE_BLOCK) + E
    starts = jnp.cumsum(counts) - counts
    padded = (counts + MOE_BLOCK - 1) // MOE_BLOCK * MOE_BLOCK
    pad_ends = jnp.cumsum(padded)
    pad_starts = pad_ends - padded
    e_ids = jnp.arange(E, dtype=jnp.int32)
    pos = jnp.sum(jnp.where(top_e[..., None] == e_ids, pad_starts, 0), axis=-1) + rank
    blk_first = jnp.arange(n_blocks, dtype=jnp.int32) * MOE_BLOCK
    blk_expert = jnp.minimum(jnp.sum((pad_ends[None, :] <= blk_first[:, None]).astype(jnp.int32), axis=1),
                             E - 1)
    n_used = (pad_ends[-1:] // MOE_BLOCK).astype(jnp.int32)
    tok = jnp.arange(T, dtype=jnp.int32)
    order = jnp.sort((top_e * T + tok[None, :]).reshape(n_assign)) % T
    row_j = (blk_first - pad_starts[blk_expert])[:, None] + jnp.arange(MOE_BLOCK, dtype=jnp.int32)[None, :]
    row_src = jnp.clip(starts[blk_expert][:, None] + row_j, 0, n_assign - 1)
    row_tok = jnp.where(row_j < counts[blk_expert][:, None], order[row_src], 0).reshape(-1)

    x_rows = x16.at[row_tok].get(mode="promise_in_bounds")
    y_rows = _grouped_swiglu(x_rows, w_gate_up, w_down, layer, blk_expert, n_used, MOE_BLOCK, BF16,
                             "routed_experts")
    y_tok = y_rows.at[pos.reshape(n_assign)].get(mode="promise_in_bounds").reshape(K, T, D)
    return _moe_combine(x, shared, y_tok, gate.T, ln_g, ln_b)


def kernel(x, positions, mla_w_in, mla_q_norm, mla_kv_norm, mla_w_qb, mla_w_kvb, mla_w_o, gdn_w_in, gdn_conv_w, gdn_a_log, gdn_dt_bias, gdn_norm, gdn_w_out, ln_mix_g, ln_mix_b, ln_ffn_g, ln_ffn_b, moe_router, moe_router_bias, moe_w_gate_up, moe_w_down, moe_shared_gate_up, moe_shared_down):
    B, S, D = x.shape
    x = x.reshape(B * S, D)
    x16 = x.astype(BF16)
    for i in range(DEPTH):
        j = i // 2
        if i % 2 == 0:
            x, x16 = _mla_mixer(x, x16, positions, mla_w_in[j], mla_q_norm[j], mla_kv_norm[j],
                                mla_w_qb[j], mla_w_kvb[j], mla_w_o[j], ln_mix_g[i], ln_mix_b[i])
        else:
            x, x16 = _gdn_mixer(x, x16, B, gdn_w_in[j], gdn_conv_w[j], gdn_a_log[j], gdn_dt_bias[j],
                                gdn_norm[j], gdn_w_out[j], ln_mix_g[i], ln_mix_b[i])
        x, x16 = _moe_ffn(x, x16, i, moe_router[i], moe_router_bias[i], moe_w_gate_up, moe_w_down,
                          moe_shared_gate_up[:, None], moe_shared_down[:, None], ln_ffn_g[i], ln_ffn_b[i])
    return x.reshape(B, S, D)
```

```python
import functools
import math

import jax
import jax.numpy as jnp
from jax import lax
from jax.experimental import pallas as pl
from jax.experimental.pallas import tpu as pltpu

F32 = jnp.float32
BF16 = jnp.bfloat16

DEPTH = 2
MLA_HEADS = 16
MLA_Q_RANK = 512
MLA_KV_RANK = 512
MLA_NOPE_DIM = 128
MLA_ROPE_DIM = 64
MLA_V_DIM = 128
MLA_QK_DIM = MLA_NOPE_DIM + MLA_ROPE_DIM
MLA_SCALE = MLA_QK_DIM ** -0.5
ROPE_THETA = 10000.0

GDN_K_HEADS = 16
GDN_V_HEADS = 32
GDN_K_DIM = 128
GDN_V_DIM = 128
CONV_K = 4
CHUNK = 64
GDN_QK_WIDTH = GDN_K_HEADS * GDN_K_DIM
GDN_V_WIDTH = GDN_V_HEADS * GDN_V_DIM
GDN_CONV_DIM = 2 * GDN_QK_WIDTH + GDN_V_WIDTH

N_EXPERTS = 64
TOP_K = 8
N_GROUPS = 8
TOPK_GROUPS = 4
EXPERT_DIM = 512
ROUTED_SCALE = 2.5
MOE_BLOCK = 256

ALPHA = (2 * DEPTH) ** 0.25
LN_EPS = 1e-5
RMS_EPS = 1e-6

V7X_VMEM_BYTES = 64 * 1024 * 1024
VMEM_LIMIT_CAP = 56 * 1024 * 1024
NEG_BIG = -1e30


def _vmem_limit(nbytes):
    return int(min(max(nbytes, 16 * 1024 * 1024), VMEM_LIMIT_CAP))


def _nbytes(shape, dtype):
    return math.prod(shape) * jnp.dtype(dtype).itemsize


def _layer_norm_rows(y, g, b):
    mu = jnp.mean(y, axis=-1, keepdims=True)
    yc = y - mu
    var = jnp.mean(yc * yc, axis=-1, keepdims=True)
    return yc * lax.rsqrt(var + LN_EPS) * g + b


def _pack_halves(x):
    n = x.shape[-1] // 2
    lo = pltpu.bitcast(x[:, :n].astype(BF16).astype(F32), jnp.uint32)
    hi = pltpu.bitcast(x[:, n:].astype(BF16).astype(F32), jnp.uint32)
    return hi | (lo >> 16)


def _unpack_halves(w):
    lo = pltpu.bitcast(w << 16, F32)
    hi = pltpu.bitcast(w & jnp.uint32(0xFFFF0000), F32)
    return lo, hi


def _mm_kernel(x_ref, w_ref, o_ref):
    o_ref[...] = jnp.dot(x_ref[...], w_ref[...], preferred_element_type=F32).astype(o_ref.dtype)


def _matmul(x, w, out_dtype, tm=1024, tn=1024, name="matmul"):
    M, K = x.shape
    N = w.shape[1]
    tm = min(tm, M)
    tn = min(tn, N)
    assert M % tm == 0 and N % tn == 0, (M, N, tm, tn)
    need = 2 * (_nbytes((tm, K), x.dtype) + _nbytes((K, tn), w.dtype) + _nbytes((tm, tn), out_dtype))
    need += _nbytes((tm, tn), F32)
    return pl.pallas_call(
        _mm_kernel,
        grid=(N // tn, M // tm),
        in_specs=[pl.BlockSpec((tm, K), lambda j, i: (i, 0)),
                  pl.BlockSpec((K, tn), lambda j, i: (0, j))],
        out_specs=pl.BlockSpec((tm, tn), lambda j, i: (i, j)),
        out_shape=jax.ShapeDtypeStruct((M, N), out_dtype),
        compiler_params=pltpu.CompilerParams(
            dimension_semantics=("parallel", "parallel"),
            vmem_limit_bytes=_vmem_limit(need + (4 << 20))),
        name=name,
    )(x, w)


def _mm_ln_kernel(x_ref, w_ref, r_ref, g_ref, b_ref, o_ref, opk_ref):
    y = ALPHA * r_ref[...] + jnp.dot(x_ref[...], w_ref[...], preferred_element_type=F32)
    out = _layer_norm_rows(y, g_ref[...], b_ref[...])
    o_ref[...] = out
    opk_ref[...] = _pack_halves(out)


def _matmul_ln(x, w, resid, ln_g, ln_b, tm, name):
    M, K = x.shape
    N = w.shape[1]
    tm = min(tm, M)
    assert M % tm == 0
    need = (_nbytes((K, N), BF16) + 2 * _nbytes((tm, K), BF16) + 4 * _nbytes((tm, N), F32)
            + 2 * _nbytes((tm, N), BF16) + 2 * _nbytes((tm, N), F32))
    return pl.pallas_call(
        _mm_ln_kernel,
        grid=(M // tm,),
        in_specs=[pl.BlockSpec((tm, K), lambda i: (i, 0)),
                  pl.BlockSpec((K, N), lambda i: (0, 0), pipeline_mode=pl.Buffered(1)),
                  pl.BlockSpec((tm, N), lambda i: (i, 0)),
                  pl.BlockSpec((1, N), lambda i: (0, 0)),
                  pl.BlockSpec((1, N), lambda i: (0, 0))],
        out_specs=[pl.BlockSpec((tm, N), lambda i: (i, 0)),
                   pl.BlockSpec((tm, N // 2), lambda i: (i, 0))],
        out_shape=[jax.ShapeDtypeStruct((M, N), F32), jax.ShapeDtypeStruct((M, N // 2), jnp.uint32)],
        compiler_params=pltpu.CompilerParams(
            dimension_semantics=("parallel",),
            vmem_limit_bytes=_vmem_limit(need + (4 << 20))),
        name=name,
    )(x, w, resid, ln_g.reshape(1, N), ln_b.reshape(1, N))


def _attn_kernel(q_ref, k_ref, v_ref, o_ref, *, blk):
    i = pl.program_id(2)
    wide = 2
    q = q_ref[0, 0]

    def step(carry, off, width, mask):
        m, l, acc = carry
        k = k_ref[0, 0, pl.ds(off, width), :]
        s = lax.dot_general(q, k, (((1,), (1,)), ((), ())), preferred_element_type=F32)
        if mask is not None:
            s = jnp.where(mask, s, NEG_BIG)
        m_new = jnp.maximum(m, jnp.max(s, axis=-1, keepdims=True))
        alpha = jnp.exp(m - m_new)
        p = jnp.exp(s - m_new)
        l = alpha * l + jnp.sum(p, axis=-1, keepdims=True)
        v = v_ref[0, 0, pl.ds(off, width), :]
        acc = alpha * acc + jnp.dot(p.astype(BF16), v, preferred_element_type=F32)
        return m_new, l, acc

    carry = (jnp.full((blk, 1), NEG_BIG, F32), jnp.zeros((blk, 1), F32), jnp.zeros((blk, MLA_V_DIM), F32))
    n_wide = i // wide
    carry = lax.fori_loop(
        0, n_wide, lambda j, c: step(c, pl.multiple_of(j * (wide * blk), wide * blk), wide * blk, None), carry)
    carry = lax.fori_loop(
        n_wide * wide, i, lambda j, c: step(c, pl.multiple_of(j * blk, blk), blk, None), carry)
    row = lax.broadcasted_iota(jnp.int32, (blk, blk), 0)
    col = lax.broadcasted_iota(jnp.int32, (blk, blk), 1)
    m, l, acc = step(carry, pl.multiple_of(i * blk, blk), blk, row >= col)
    o_ref[0] = (acc / l).astype(o_ref.dtype)


def _attention(q, k, v, blk=512):
    B, H, S, qk_w = q.shape
    blk = min(blk, S)
    assert S % blk == 0
    need = 2 * (_nbytes((blk, 256), BF16) + _nbytes((S, 256), BF16) + _nbytes((S, 128), BF16)
                + _nbytes((blk, 128), BF16)) + 6 * _nbytes((blk, blk), F32)
    return pl.pallas_call(
        functools.partial(_attn_kernel, blk=blk),
        grid=(B, H, S // blk),
        in_specs=[pl.BlockSpec((1, 1, blk, qk_w), lambda b, h, i: (b, h, i, 0)),
                  pl.BlockSpec((1, 1, S, qk_w), lambda b, h, i: (b, h, 0, 0)),
                  pl.BlockSpec((1, 1, S, MLA_V_DIM), lambda b, h, i: (b, h, 0, 0))],
        out_specs=pl.BlockSpec((1, blk, MLA_V_DIM), lambda b, h, i: (b, i, h)),
        out_shape=jax.ShapeDtypeStruct((B, S, H * MLA_V_DIM), BF16),
        compiler_params=pltpu.CompilerParams(
            dimension_semantics=("parallel", "parallel", "parallel"),
            vmem_limit_bytes=_vmem_limit(need + (8 << 20))),
        name="mla_attention",
    )(q, k, v)


def _gdn_kernel(pq_ref, pk_ref, pv_ref, pz_ref, hq_ref, hk_ref, hv_ref, cq_ref, ck_ref, cv_ref,
                g_ref, b_ref, ng_ref, o_ref, state_ref, *, n_chunks, k_heads):
    C = CHUNK
    rep = GDN_V_HEADS // GDN_K_HEADS
    v_heads = k_heads * rep
    DK, DV = GDN_K_DIM, GDN_V_DIM
    first_block = pl.program_id(2) == 0

    @pl.when(first_block)
    def _():
        state_ref[...] = jnp.zeros_like(state_ref)

    row = lax.broadcasted_iota(jnp.int32, (C, C), 0)
    col = lax.broadcasted_iota(jnp.int32, (C, C), 1)
    causal = row >= col
    strict = row > col
    eye = row == col
    eye_f = eye.astype(F32)
    nt = (((1,), (1,)), ((), ()))
    tn = (((0,), (0,)), ((), ()))

    def dot(a, b):
        return jnp.dot(a, b, preferred_element_type=F32)

    def conv_silu(x_ref, halo_ref, w_ref, c, r0):
        cur = x_ref[0, pl.ds(r0, C), :]
        prev_start = pl.multiple_of(jnp.maximum(c * (C // 8) - 1, 0) * 8, 8)
        prev_rows = x_ref[0, pl.ds(prev_start, 8), :]
        halo = jnp.where(first_block, 0.0, halo_ref[0])
        ext = jnp.concatenate([jnp.where(c == 0, halo, prev_rows), cur], axis=0)
        acc = cur * w_ref[CONV_K - 1:CONV_K, :]
        for j in range(1, CONV_K):
            acc = acc + pltpu.roll(ext, j, axis=0)[8:] * w_ref[CONV_K - 1 - j:CONV_K - j, :]
        return acc * jax.nn.sigmoid(acc)

    def l2n(t):
        return t * lax.rsqrt(jnp.sum(t * t, axis=-1, keepdims=True) + RMS_EPS)

    def chunk(c, carry):
        r0 = pl.multiple_of(c * C, C)
        qa = conv_silu(pq_ref, hq_ref, cq_ref, c, r0)
        ka = conv_silu(pk_ref, hk_ref, ck_ref, c, r0)
        va = conv_silu(pv_ref, hv_ref, cv_ref, c, r0)
        q = [l2n(qa[:, i * DK:(i + 1) * DK]) * (DK ** -0.5) for i in range(k_heads)]
        k = [l2n(ka[:, i * DK:(i + 1) * DK]) for i in range(k_heads)]
        v = [va[:, h * DV:(h + 1) * DV] for h in range(v_heads)]
        k16 = [t.astype(BF16) for t in k]
        qk16 = [jnp.concatenate([q[i], k[i]], axis=0).astype(BF16) for i in range(k_heads)]
        qkk = [lax.dot_general(qk16[i], k16[i], nt, preferred_element_type=F32) for i in range(k_heads)]

        g_row = [g_ref[0, h, pl.ds(c, 1), :] for h in range(v_heads)]
        b_row = [b_ref[0, h, pl.ds(c, 1), :] for h in range(v_heads)]
        g_col = [jnp.sum(jnp.where(eye, t, 0.0), axis=1, keepdims=True) for t in g_row]
        b_col = [jnp.sum(jnp.where(eye, t, 0.0), axis=1, keepdims=True) for t in b_row]
        g_last = [t[:, C - 1:C] for t in g_row]
        decay = [jnp.exp(jnp.where(causal, g_col[h] - g_row[h], NEG_BIG)) for h in range(v_heads)]
        a = [jnp.where(strict, qkk[h // rep][C:] * b_col[h] * decay[h], 0.0) for h in range(v_heads)]
        qkd = [(qkk[h // rep][:C] * decay[h]).astype(BF16) for h in range(v_heads)]

        a16 = [t.astype(BF16) for t in a]
        x = [dot(t, t) for t in a16]
        p = [eye_f - t for t in a]
        for _ in range(int(math.log2(C)) - 2):
            x16 = [t.astype(BF16) for t in x]
            px = [dot(jnp.concatenate([p[h].astype(BF16), x16[h]], axis=0), x16[h]) for h in range(v_heads)]
            p = [p[h] + px[h][:C] for h in range(v_heads)]
            x = [px[h][C:] for h in range(v_heads)]
        p = [p[h] + dot(p[h].astype(BF16), x[h].astype(BF16)) for h in range(v_heads)]

        eg = [jnp.exp(t) for t in g_col]
        rhs = [jnp.concatenate([v[h] * b_col[h], k[h // rep] * (b_col[h] * eg[h])], axis=1).astype(BF16)
               for h in range(v_heads)]
        sol = [dot(p[h].astype(BF16), rhs[h]) for h in range(v_heads)]
        s = [state_ref[h] for h in range(v_heads)]
        wq = [jnp.concatenate([sol[h][:, DV:], q[h // rep] * eg[h]], axis=0).astype(BF16)
              for h in range(v_heads)]
        ws = [dot(wq[h], s[h].astype(BF16)) for h in range(v_heads)]
        u16 = [(sol[h][:, :DV] - ws[h][:C]).astype(BF16) for h in range(v_heads)]
        k_dec = [(k[h // rep] * jnp.exp(g_last[h] - g_col[h])).astype(BF16) for h in range(v_heads)]
        ds = [lax.dot_general(k_dec[h], u16[h], tn, preferred_element_type=F32) for h in range(v_heads)]
        o = [ws[h][C:] + dot(qkd[h], u16[h]) for h in range(v_heads)]
        for h in range(v_heads):
            state_ref[h] = s[h] * jnp.exp(g_last[h]) + ds[h]
            z = pz_ref[0, pl.ds(r0, C), h * DV:(h + 1) * DV]
            on = o[h] * lax.rsqrt(jnp.mean(o[h] * o[h], axis=-1, keepdims=True) + RMS_EPS) * ng_ref[...]
            o_ref[0, pl.ds(r0, C), h * DV:(h + 1) * DV] = (on * (z * jax.nn.sigmoid(z))).astype(o_ref.dtype)
        return carry

    lax.fori_loop(0, n_chunks, chunk, 0)


def _gated_delta_rule(p, conv_w, g_cum, beta, norm_g, n_chunks=8, k_heads=8):
    B, S, _ = p.shape
    n_chunks = min(n_chunks, S // CHUNK)
    sb = n_chunks * CHUNK
    rep = GDN_V_HEADS // GDN_K_HEADS
    v_heads = k_heads * rep
    wq, wv = k_heads * GDN_K_DIM, v_heads * GDN_V_DIM
    assert S % sb == 0 and GDN_K_HEADS % k_heads == 0
    k_off, v_off, z_off = GDN_QK_WIDTH // wq, 2 * GDN_QK_WIDTH // wv, GDN_CONV_DIM // wv
    hb = sb // 8

    def main(width, off):
        return pl.BlockSpec((1, sb, width), lambda b, h, s: (b, s, off + h))

    def halo(width, off):
        return pl.BlockSpec((1, 8, width), lambda b, h, s: (b, jnp.maximum(s * hb - 1, 0), off + h))

    def cw(width, off):
        return pl.BlockSpec((CONV_K, width), lambda b, h, s: (0, off + h))

    blocks = 2 * sb * (2 * wq + 2 * wv) * 4 + 2 * sb * wv * 2
    return pl.pallas_call(
        functools.partial(_gdn_kernel, n_chunks=n_chunks, k_heads=k_heads),
        grid=(B, GDN_K_HEADS // k_heads, S // sb),
        in_specs=[main(wq, 0), main(wq, k_off), main(wv, v_off), main(wv, z_off),
                  halo(wq, 0), halo(wq, k_off), halo(wv, v_off),
                  cw(wq, 0), cw(wq, k_off), cw(wv, v_off),
                  pl.BlockSpec((1, v_heads, n_chunks, CHUNK), lambda b, h, s: (b, h, s, 0)),
                  pl.BlockSpec((1, v_heads, n_chunks, CHUNK), lambda b, h, s: (b, h, s, 0)),
                  pl.BlockSpec((1, GDN_V_DIM), lambda b, h, s: (0, 0))],
        out_specs=pl.BlockSpec((1, sb, wv), lambda b, h, s: (b, s, h)),
        out_shape=jax.ShapeDtypeStruct((B, S, GDN_V_WIDTH), BF16),
        scratch_shapes=[pltpu.VMEM((v_heads, GDN_K_DIM, GDN_V_DIM), F32)],
        compiler_params=pltpu.CompilerParams(
            dimension_semantics=("parallel", "parallel", "arbitrary"),
            vmem_limit_bytes=_vmem_limit(blocks + (12 << 20))),
        name="gated_delta_rule",
    )(p, p, p, p, p, p, p, conv_w, conv_w, conv_w, g_cum, beta, norm_g)


def _expert_kernel(be_ref, nu_ref, x_ref, wgu_ref, wd_ref, o_ref, wgu16_ref, wd16_ref):
    i = pl.program_id(0)
    used = i < nu_ref[0]
    new_expert = jnp.logical_or(i == 0, be_ref[i] != be_ref[jnp.maximum(i - 1, 0)])

    @pl.when(jnp.logical_and(used, new_expert))
    def _():
        wgu16_ref[...] = wgu_ref[0, 0].astype(BF16)
        wd16_ref[...] = wd_ref[0, 0].astype(BF16)

    @pl.when(used)
    def _():
        x_lo, x_hi = _unpack_halves(x_ref[...])
        half = x_lo.shape[-1]
        gu = (jnp.dot(x_lo.astype(BF16), wgu16_ref[:half, :], preferred_element_type=F32)
              + jnp.dot(x_hi.astype(BF16), wgu16_ref[half:, :], preferred_element_type=F32))
        f = gu.shape[-1] // 2
        gate = gu[:, :f]
        act = gate * jax.nn.sigmoid(gate) * gu[:, f:]
        y = jnp.dot(act.astype(BF16), wd16_ref[...], preferred_element_type=F32)
        if o_ref.dtype == jnp.uint32:
            o_ref[...] = _pack_halves(y)
        else:
            o_ref[...] = y.astype(o_ref.dtype)

    @pl.when(jnp.logical_not(used))
    def _():
        o_ref[...] = jnp.zeros_like(o_ref)


def _grouped_swiglu(x_rows, w_gate_up, w_down, layer, blk_expert, n_used, tm, out_dtype, name):
    R = x_rows.shape[0]
    D = w_gate_up.shape[-2]
    F2 = w_gate_up.shape[-1]
    tm = min(tm, R)
    assert R % tm == 0 and x_rows.shape[1] * 2 == D
    d_out = D // 2 if out_dtype == jnp.uint32 else D
    w_elems = D * F2 + (F2 // 2) * D
    need = (2 * w_elems * 4 + w_elems * 2 + 2 * _nbytes((tm, D), BF16) + 2 * _nbytes((tm, D), F32)
            + 2 * _nbytes((tm, F2), F32) + _nbytes((tm, D), F32))
    grid_spec = pltpu.PrefetchScalarGridSpec(
        num_scalar_prefetch=2,
        grid=(R // tm,),
        in_specs=[pl.BlockSpec((tm, D // 2), lambda i, be, nu: (i, 0)),
                  pl.BlockSpec((1, 1, D, F2), lambda i, be, nu: (layer, be[i], 0, 0)),
                  pl.BlockSpec((1, 1, F2 // 2, D), lambda i, be, nu: (layer, be[i], 0, 0))],
        out_specs=pl.BlockSpec((tm, d_out), lambda i, be, nu: (i, 0)),
        scratch_shapes=[pltpu.VMEM((D, F2), BF16), pltpu.VMEM((F2 // 2, D), BF16)],
    )
    return pl.pallas_call(
        _expert_kernel,
        grid_spec=grid_spec,
        out_shape=jax.ShapeDtypeStruct((R, d_out), out_dtype),
        compiler_params=pltpu.CompilerParams(
            dimension_semantics=("arbitrary",),
            vmem_limit_bytes=_vmem_limit(need + (4 << 20))),
        name=name,
    )(blk_expert, n_used, x_rows, w_gate_up, w_down)


def _combine_kernel(x_ref, sh_ref, y_ref, gate_ref, g_ref, b_ref, o_ref, o16_ref):
    half = x_ref.shape[-1] // 2
    acc = ALPHA * x_ref[...] + sh_ref[...]
    acc_lo, acc_hi = acc[:, :half], acc[:, half:]
    gate = gate_ref[...]
    for k in range(TOP_K):
        y_lo, y_hi = _unpack_halves(y_ref[k])
        acc_lo = acc_lo + gate[:, k:k + 1] * y_lo
        acc_hi = acc_hi + gate[:, k:k + 1] * y_hi
    out = _layer_norm_rows(jnp.concatenate([acc_lo, acc_hi], axis=1), g_ref[...], b_ref[...])
    o_ref[...] = out
    o16_ref[...] = out.astype(BF16)


def _moe_combine(x, shared, y_tok, gate, ln_g, ln_b, tm=256):
    T, D = x.shape
    tm = min(tm, T)
    assert T % tm == 0
    need = 2 * (3 * _nbytes((tm, D), F32) + _nbytes((tm, D), BF16) + _nbytes((tm, TOP_K * D), BF16)) \
        + 3 * _nbytes((tm, D), F32)
    return pl.pallas_call(
        _combine_kernel,
        grid=(T // tm,),
        in_specs=[pl.BlockSpec((tm, D), lambda i: (i, 0)),
                  pl.BlockSpec((tm, D), lambda i: (i, 0)),
                  pl.BlockSpec((TOP_K, tm, D // 2), lambda i: (0, i, 0)),
                  pl.BlockSpec((tm, TOP_K), lambda i: (i, 0)),
                  pl.BlockSpec((1, D), lambda i: (0, 0)),
                  pl.BlockSpec((1, D), lambda i: (0, 0))],
        out_specs=[pl.BlockSpec((tm, D), lambda i: (i, 0)),
                   pl.BlockSpec((tm, D), lambda i: (i, 0))],
        out_shape=[jax.ShapeDtypeStruct((T, D), F32), jax.ShapeDtypeStruct((T, D), BF16)],
        compiler_params=pltpu.CompilerParams(
            dimension_semantics=("parallel",),
            vmem_limit_bytes=_vmem_limit(need + (4 << 20))),
        name="moe_combine",
    )(x, shared, y_tok, gate, ln_g.reshape(1, D), ln_b.reshape(1, D))


MLA_QK_PAD = 2 * MLA_NOPE_DIM
MLA_IN_PAD = MLA_Q_RANK + MLA_KV_RANK + 2 * MLA_NOPE_DIM


def _mla_proj_kernel(x_ref, win_ref, wq_ref, wkv_ref, qg_ref, kvg_ref, csq_ref, ck_ref, sk_ref,
                     q_ref, k_ref, v_ref):
    NP = MLA_NOPE_DIM
    c = jnp.dot(x_ref[...], win_ref[...], preferred_element_type=F32)

    def rms(t, g):
        return (t * lax.rsqrt(jnp.mean(t * t, axis=-1, keepdims=True) + RMS_EPS) * g).astype(BF16)

    cq = rms(c[:, :MLA_Q_RANK], qg_ref[...])
    ckv = rms(c[:, MLA_Q_RANK:MLA_Q_RANK + MLA_KV_RANK], kvg_ref[...])
    o = MLA_Q_RANK + MLA_KV_RANK
    k_rope = (c[:, o:o + NP] * ck_ref[...] + c[:, o + NP:o + 2 * NP] * sk_ref[...]).astype(BF16)
    low = lax.broadcasted_iota(jnp.int32, (x_ref.shape[0], NP), 1) < MLA_ROPE_DIM
    csq = csq_ref[...]
    for h in range(MLA_HEADS):
        qh = jnp.dot(cq, wq_ref[:, h * 2 * NP:(h + 1) * 2 * NP], preferred_element_type=F32)
        u = qh[:, NP:] * csq
        q_rope = jnp.where(low, u + pltpu.roll(u, MLA_ROPE_DIM, axis=1), 0.0)
        q_ref[0, h, :, :NP] = (qh[:, :NP] * MLA_SCALE).astype(BF16)
        q_ref[0, h, :, NP:] = (q_rope * MLA_SCALE).astype(BF16)
        kvh = jnp.dot(ckv, wkv_ref[:, h * 2 * NP:(h + 1) * 2 * NP], preferred_element_type=F32)
        k_ref[0, h, :, :NP] = kvh[:, :NP].astype(BF16)
        k_ref[0, h, :, NP:] = k_rope
        v_ref[0, h] = kvh[:, NP:].astype(BF16)


def _mla_proj(x16, w_in_ext, w_q_ext, w_kvb, q_norm_g, kv_norm_g, csq, ck, sk, batch, tm=256):
    T, D = x16.shape
    S = T // batch
    H = MLA_HEADS
    tm = min(tm, S)
    assert S % tm == 0
    nsb = S // tm
    w_bytes = (w_in_ext.size + w_q_ext.size + w_kvb.size) * 2
    out_bytes = _nbytes((H, tm, 2 * MLA_QK_PAD + MLA_V_DIM), BF16)
    need = w_bytes + 2 * _nbytes((tm, D), BF16) + 2 * out_bytes + 4 * _nbytes((tm, MLA_IN_PAD), F32)

    def const(shape):
        return pl.BlockSpec(shape, lambda i: (0, 0), pipeline_mode=pl.Buffered(1))

    def rows(width):
        return pl.BlockSpec((tm, width), lambda i: (i, 0))

    def heads(width):
        return pl.BlockSpec((1, H, tm, width), lambda i: (i // nsb, 0, i % nsb, 0))

    return pl.pallas_call(
        _mla_proj_kernel,
        grid=(T // tm,),
        in_specs=[rows(D), const(w_in_ext.shape), const(w_q_ext.shape), const(w_kvb.shape),
                  const((1, MLA_Q_RANK)), const((1, MLA_KV_RANK)),
                  rows(MLA_NOPE_DIM), rows(MLA_NOPE_DIM), rows(MLA_NOPE_DIM)],
        out_specs=[heads(MLA_QK_PAD), heads(MLA_QK_PAD), heads(MLA_V_DIM)],
        out_shape=[jax.ShapeDtypeStruct((batch, H, S, MLA_QK_PAD), BF16),
                   jax.ShapeDtypeStruct((batch, H, S, MLA_QK_PAD), BF16),
                   jax.ShapeDtypeStruct((batch, H, S, MLA_V_DIM), BF16)],
        compiler_params=pltpu.CompilerParams(
            dimension_semantics=("parallel",),
            vmem_limit_bytes=_vmem_limit(need + (6 << 20))),
        name="mla_proj",
    )(x16, w_in_ext, w_q_ext, w_kvb, q_norm_g.reshape(1, -1), kv_norm_g.reshape(1, -1), csq, ck, sk)


def _router_kernel(x_ref, wt_ref, bias_ref, e_ref, gate_ref, rank_ref, cnt_ref, carry_ref):
    E, G, K = N_EXPERTS, N_GROUPS, TOP_K
    P = E // G
    tm = x_ref.shape[0]

    @pl.when(pl.program_id(0) == 0)
    def _():
        carry_ref[...] = jnp.zeros_like(carry_ref)

    logits = lax.dot_general(wt_ref[...], x_ref[...], (((1,), (1,)), ((), ())),
                             precision=lax.Precision.HIGHEST, preferred_element_type=F32)
    scores = jax.nn.sigmoid(logits)
    biased = scores + bias_ref[...]

    def preceding(v):
        n = v.shape[0]
        idx = lax.broadcasted_iota(jnp.int32, (n, 1), 0)
        cnt = jnp.zeros(v.shape, F32)
        for j in range(n):
            row = v[j:j + 1, :]
            tie = jnp.where(idx > j, 1.0, 0.0)
            cnt = cnt + jnp.where(row > v, 1.0, jnp.where(row == v, tie, 0.0))
        return cnt

    blocks = [biased[g * P:(g + 1) * P, :] for g in range(G)]
    group_score = jnp.concatenate(
        [jnp.sum(jnp.where(preceding(b) < 2.0, b, 0.0), axis=0, keepdims=True) for b in blocks], axis=0)
    group_sel = preceding(group_score) < float(TOPK_GROUPS)
    masked = jnp.concatenate(
        [jnp.where(group_sel[g:g + 1, :], blocks[g], -jnp.inf) for g in range(G)], axis=0)
    sel = preceding(masked) < float(K)
    sel_f = jnp.where(sel, 1.0, 0.0)
    gate = jnp.where(sel, scores, 0.0)
    gate = gate / (jnp.sum(gate, axis=0, keepdims=True) + 1e-20) * ROUTED_SCALE

    t_row = lax.broadcasted_iota(jnp.int32, (tm, tm), 0)
    t_col = lax.broadcasted_iota(jnp.int32, (tm, tm), 1)
    before_t = jnp.where(t_row < t_col, 1.0, 0.0).astype(BF16)
    sel16 = sel_f.astype(BF16)
    rank = jnp.dot(sel16, before_t, preferred_element_type=F32) + carry_ref[...]
    carry_ref[...] = carry_ref[...] + jnp.sum(sel_f, axis=1, keepdims=True)
    e_row = lax.broadcasted_iota(jnp.int32, (E, E), 0)
    e_col = lax.broadcasted_iota(jnp.int32, (E, E), 1)
    before_e = jnp.where(e_col < e_row, 1.0, 0.0).astype(BF16)
    slot = jnp.where(sel, jnp.dot(before_e, sel16, preferred_element_type=F32), -1.0)
    e_idx = lax.broadcasted_iota(jnp.int32, (E, 1), 0).astype(F32)

    def pick(k, t):
        return jnp.sum(jnp.where(slot == float(k), t, 0.0), axis=0, keepdims=True)

    e_ref[...] = jnp.concatenate([pick(k, e_idx) for k in range(K)], axis=0).astype(jnp.int32)
    rank_ref[...] = jnp.concatenate([pick(k, rank) for k in range(K)], axis=0).astype(jnp.int32)
    gate_ref[...] = jnp.concatenate([pick(k, gate) for k in range(K)], axis=0)
    cnt_ref[...] = jnp.broadcast_to(carry_ref[...], cnt_ref.shape).astype(jnp.int32)


def _router(x, router_wt, router_bias, tm=512):
    T, D = x.shape
    E = router_wt.shape[0]
    tm = min(tm, T)
    assert T % tm == 0
    kt = pl.BlockSpec((TOP_K, tm), lambda i: (0, i))
    top_e, gate, rank, cnt = pl.pallas_call(
        _router_kernel,
        grid=(T // tm,),
        in_specs=[pl.BlockSpec((tm, D), lambda i: (i, 0)),
                  pl.BlockSpec((E, D), lambda i: (0, 0)),
                  pl.BlockSpec((E, 1), lambda i: (0, 0))],
        out_specs=[kt, kt, kt, pl.BlockSpec((E, 128), lambda i: (0, 0))],
        out_shape=[jax.ShapeDtypeStruct((TOP_K, T), jnp.int32), jax.ShapeDtypeStruct((TOP_K, T), F32),
                   jax.ShapeDtypeStruct((TOP_K, T), jnp.int32), jax.ShapeDtypeStruct((E, 128), jnp.int32)],
        scratch_shapes=[pltpu.VMEM((E, 1), F32)],
        compiler_params=pltpu.CompilerParams(
            dimension_semantics=("arbitrary",),
            vmem_limit_bytes=_vmem_limit(2 * _nbytes((tm, D), F32) + (16 << 20))),
        name="moe_router",
    )(x, router_wt, router_bias.reshape(E, 1))
    return top_e, gate, rank, cnt[:, 0]


def _mla_mixer(x, x16, positions, w_in, q_norm_g, kv_norm_g, w_qb, w_kvb, w_o, ln_g, ln_b):
    B, S = positions.shape
    T, D = x.shape
    H = MLA_HEADS
    half = MLA_ROPE_DIM // 2

    def rot(w):
        return jnp.concatenate([-w[..., half:], w[..., :half]], axis=-1)

    lat = MLA_Q_RANK + MLA_KV_RANK
    w_kr = w_in[:, lat:]
    zpad = jnp.zeros((D, MLA_NOPE_DIM - MLA_ROPE_DIM), w_in.dtype)
    w_in_ext = jnp.concatenate([w_in[:, :lat], w_kr, zpad, rot(w_kr), zpad], axis=1).astype(BF16)
    w_q3 = w_qb.reshape(MLA_Q_RANK, H, MLA_QK_DIM)
    w_q_rope = w_q3[..., MLA_NOPE_DIM:]
    w_q_ext = jnp.concatenate([w_q3[..., :MLA_NOPE_DIM], w_q_rope, rot(w_q_rope)], axis=-1)
    w_q_ext = w_q_ext.reshape(MLA_Q_RANK, H * MLA_QK_PAD).astype(BF16)
    inv_freq = 1.0 / (ROPE_THETA ** (jnp.arange(0, MLA_ROPE_DIM, 2, dtype=F32) / MLA_ROPE_DIM))
    ang = positions.astype(F32).reshape(T, 1) * inv_freq
    cos, sin = jnp.cos(ang), jnp.sin(ang)
    zero = jnp.zeros_like(cos)
    csq = jnp.concatenate([cos, cos, sin, sin], axis=1)
    ck = jnp.concatenate([cos, cos, zero, zero], axis=1)
    sk = jnp.concatenate([sin, sin, zero, zero], axis=1)
    q_cat, k_cat, v = _mla_proj(x16, w_in_ext, w_q_ext, w_kvb.astype(BF16), q_norm_g, kv_norm_g,
                                csq, ck, sk, B)
    o = _attention(q_cat, k_cat, v)
    return _matmul_ln(o.reshape(T, H * MLA_V_DIM), w_o.astype(BF16), x, ln_g, ln_b, 512, "mla_out_ln")


def _gdn_mixer(x, x16, batch, w_in, conv_w, a_log, dt_bias, norm_g, w_out, ln_g, ln_b):
    T, D = x.shape
    B, S = batch, T // batch
    n_main = GDN_CONV_DIM + GDN_V_WIDTH
    p = _matmul(x16, w_in[:, :n_main].astype(BF16), F32, name="gdn_in")
    ba = _matmul(x16, w_in[:, n_main:].astype(BF16), F32, name="gdn_in_ba")
    b = ba[:, :GDN_V_HEADS].reshape(B, S, GDN_V_HEADS)
    a = ba[:, GDN_V_HEADS:].reshape(B, S, GDN_V_HEADS)
    beta = jax.nn.sigmoid(b)
    g = -jnp.exp(a_log) * jax.nn.softplus(a + dt_bias)
    N = S // CHUNK
    g_cum = jnp.cumsum(g.reshape(B, N, CHUNK, GDN_V_HEADS), axis=2)
    g_cum = jnp.transpose(g_cum, (0, 3, 1, 2))
    beta = jnp.transpose(beta.reshape(B, N, CHUNK, GDN_V_HEADS), (0, 3, 1, 2))
    o = _gated_delta_rule(p.reshape(B, S, n_main), conv_w, g_cum, beta, norm_g.reshape(1, GDN_V_DIM))
    return _matmul_ln(o.reshape(T, GDN_V_WIDTH), w_out.astype(BF16), x, ln_g, ln_b, 256, "gdn_out_ln")


def _moe_ffn(x, xpk, layer, router_w, router_bias, w_gate_up, w_down, shared_gate_up, shared_down, ln_g, ln_b):
    T, D = x.shape
    E, K = N_EXPERTS, TOP_K
    top_e, gate, rank, counts = _router(x, router_w.T, router_bias)

    zero = jnp.zeros((1,), jnp.int32)
    tm_sh = min(512, T)
    shared = _grouped_swiglu(xpk, shared_gate_up, shared_down, layer,
                             jnp.zeros((T // tm_sh,), jnp.int32), zero + T // tm_sh, tm_sh, F32,
                             "shared_expert")

    n_assign = T * K
    n_blocks = -(-n_assign // MOE_BLOCK) + E
    starts = jnp.cumsum(counts) - counts
    padded = (counts + MOE_BLOCK - 1) // MOE_BLOCK * MOE_BLOCK
    pad_ends = jnp.cumsum(padded)
    pad_starts = pad_ends - padded
    e_ids = jnp.arange(E, dtype=jnp.int32)
    pos = jnp.sum(jnp.where(top_e[..., None] == e_ids, pad_starts, 0), axis=-1) + rank
    blk_first = jnp.arange(n_blocks, dtype=jnp.int32) * MOE_BLOCK
    blk_expert = jnp.minimum(jnp.sum((pad_ends[None, :] <= blk_first[:, None]).astype(jnp.int32), axis=1),
                             E - 1)
    n_used = (pad_ends[-1:] // MOE_BLOCK).astype(jnp.int32)
    tok = jnp.arange(T, dtype=jnp.int32)
    order = jnp.sort((top_e * T + tok[None, :]).reshape(n_assign)) % T
    row_j = (blk_first - pad_starts[blk_expert])[:, None] + jnp.arange(MOE_BLOCK, dtype=jnp.int32)[None, :]
    row_src = jnp.clip(starts[blk_expert][:, None] + row_j, 0, n_assign - 1)
    row_tok = jnp.where(row_j < counts[blk_expert][:, None], order[row_src], 0).reshape(-1)

    x_rows = xpk.at[row_tok].get(mode="promise_in_bounds")
    y_rows = _grouped_swiglu(x_rows, w_gate_up, w_down, layer, blk_expert, n_used, MOE_BLOCK, jnp.uint32,
                             "routed_experts")
    y_tok = y_rows.at[pos.reshape(n_assign)].get(mode="promise_in_bounds").reshape(K, T, D // 2)
    return _moe_combine(x, shared, y_tok, gate.T, ln_g, ln_b)


def kernel(x, positions, mla_w_in, mla_q_norm, mla_kv_norm, mla_w_qb, mla_w_kvb, mla_w_o, gdn_w_in, gdn_conv_w, gdn_a_log, gdn_dt_bias, gdn_norm, gdn_w_out, ln_mix_g, ln_mix_b, ln_ffn_g, ln_ffn_b, moe_router, moe_router_bias, moe_w_gate_up, moe_w_down, moe_shared_gate_up, moe_shared_down):
    B, S, D = x.shape
    x = x.reshape(B * S, D)
    x16 = x.astype(BF16)
    for i in range(DEPTH):
        j = i // 2
        if i % 2 == 0:
            x, xpk = _mla_mixer(x, x16, positions, mla_w_in[j], mla_q_norm[j], mla_kv_norm[j],
                                mla_w_qb[j], mla_w_kvb[j], mla_w_o[j], ln_mix_g[i], ln_mix_b[i])
        else:
            x, xpk = _gdn_mixer(x, x16, B, gdn_w_in[j], gdn_conv_w[j], gdn_a_log[j], gdn_dt_bias[j],
                                gdn_norm[j], gdn_w_out[j], ln_mix_g[i], ln_mix_b[i])
        x, x16 = _moe_ffn(x, xpk, i, moe_router[i], moe_router_bias[i], moe_w_gate_up, moe_w_down,
                          moe_shared_gate_up[:, None], moe_shared_down[:, None], ln_ffn_g[i], ln_ffn_b[i])
    return x.reshape(B, S, D)
```

```python
import functools
import math

import jax
import jax.numpy as jnp
from jax import lax
from jax.experimental import pallas as pl
from jax.experimental.pallas import tpu as pltpu

F32 = jnp.float32
BF16 = jnp.bfloat16

DEPTH = 2
MLA_HEADS = 16
MLA_Q_RANK = 512
MLA_KV_RANK = 512
MLA_NOPE_DIM = 128
MLA_ROPE_DIM = 64
MLA_V_DIM = 128
MLA_QK_DIM = MLA_NOPE_DIM + MLA_ROPE_DIM
MLA_SCALE = MLA_QK_DIM ** -0.5
MLA_Q_SCALE = MLA_SCALE * math.log2(math.e)
ROPE_THETA = 10000.0

GDN_K_HEADS = 16
GDN_V_HEADS = 32
GDN_K_DIM = 128
GDN_V_DIM = 128
CONV_K = 4
CHUNK = 64
GDN_QK_WIDTH = GDN_K_HEADS * GDN_K_DIM
GDN_V_WIDTH = GDN_V_HEADS * GDN_V_DIM
GDN_CONV_DIM = 2 * GDN_QK_WIDTH + GDN_V_WIDTH

N_EXPERTS = 64
TOP_K = 8
N_GROUPS = 8
TOPK_GROUPS = 4
EXPERT_DIM = 512
ROUTED_SCALE = 2.5
MOE_BLOCK = 256
MOE_GATHER_CHUNKS = 4

ALPHA = (2 * DEPTH) ** 0.25
LN_EPS = 1e-5
RMS_EPS = 1e-6

V7X_VMEM_BYTES = 64 * 1024 * 1024
VMEM_LIMIT_CAP = 56 * 1024 * 1024
NEG_BIG = -1e30


def _vmem_limit(nbytes):
    return int(min(max(nbytes, 16 * 1024 * 1024), VMEM_LIMIT_CAP))


def _nbytes(shape, dtype):
    return math.prod(shape) * jnp.dtype(dtype).itemsize


def _layer_norm_rows(y, g, b):
    mu = jnp.mean(y, axis=-1, keepdims=True)
    yc = y - mu
    var = jnp.mean(yc * yc, axis=-1, keepdims=True)
    return yc * lax.rsqrt(var + LN_EPS) * g + b


def _pack_halves(x):
    n = x.shape[-1] // 2
    lo = pltpu.bitcast(x[:, :n].astype(BF16).astype(F32), jnp.uint32)
    hi = pltpu.bitcast(x[:, n:].astype(BF16).astype(F32), jnp.uint32)
    return hi | (lo >> 16)


def _unpack_halves(w):
    lo = pltpu.bitcast(w << 16, F32)
    hi = pltpu.bitcast(w & jnp.uint32(0xFFFF0000), F32)
    return lo, hi


def _mm_kernel(x_ref, w_ref, o_ref):
    o_ref[...] = jnp.dot(x_ref[...], w_ref[...], preferred_element_type=F32).astype(o_ref.dtype)


def _matmul(x, w, out_dtype, tm=1024, tn=1024, name="matmul"):
    M, K = x.shape
    N = w.shape[1]
    tm = min(tm, M)
    tn = min(tn, N)
    assert M % tm == 0 and N % tn == 0, (M, N, tm, tn)
    need = 2 * (_nbytes((tm, K), x.dtype) + _nbytes((K, tn), w.dtype) + _nbytes((tm, tn), out_dtype))
    need += _nbytes((tm, tn), F32)
    return pl.pallas_call(
        _mm_kernel,
        grid=(N // tn, M // tm),
        in_specs=[pl.BlockSpec((tm, K), lambda j, i: (i, 0)),
                  pl.BlockSpec((K, tn), lambda j, i: (0, j))],
        out_specs=pl.BlockSpec((tm, tn), lambda j, i: (i, j)),
        out_shape=jax.ShapeDtypeStruct((M, N), out_dtype),
        compiler_params=pltpu.CompilerParams(
            dimension_semantics=("parallel", "parallel"),
            vmem_limit_bytes=_vmem_limit(need + (4 << 20))),
        name=name,
    )(x, w)


def _mm_ln_kernel(x_ref, w_ref, r_ref, g_ref, b_ref, o_ref, opk_ref):
    y = ALPHA * r_ref[...] + jnp.dot(x_ref[...], w_ref[...], preferred_element_type=F32)
    out = _layer_norm_rows(y, g_ref[...], b_ref[...])
    o_ref[...] = out
    opk_ref[...] = _pack_halves(out)


def _matmul_ln(x, w, resid, ln_g, ln_b, tm, name):
    M, K = x.shape
    N = w.shape[1]
    tm = min(tm, M)
    assert M % tm == 0
    need = (_nbytes((K, N), BF16) + 2 * _nbytes((tm, K), BF16) + 4 * _nbytes((tm, N), F32)
            + 2 * _nbytes((tm, N), BF16) + 2 * _nbytes((tm, N), F32))
    return pl.pallas_call(
        _mm_ln_kernel,
        grid=(M // tm,),
        in_specs=[pl.BlockSpec((tm, K), lambda i: (i, 0)),
                  pl.BlockSpec((K, N), lambda i: (0, 0), pipeline_mode=pl.Buffered(1)),
                  pl.BlockSpec((tm, N), lambda i: (i, 0)),
                  pl.BlockSpec((1, N), lambda i: (0, 0)),
                  pl.BlockSpec((1, N), lambda i: (0, 0))],
        out_specs=[pl.BlockSpec((tm, N), lambda i: (i, 0)),
                   pl.BlockSpec((tm, N // 2), lambda i: (i, 0))],
        out_shape=[jax.ShapeDtypeStruct((M, N), F32), jax.ShapeDtypeStruct((M, N // 2), jnp.uint32)],
        compiler_params=pltpu.CompilerParams(
            dimension_semantics=("parallel",),
            vmem_limit_bytes=_vmem_limit(need + (4 << 20))),
        name=name,
    )(x, w, resid, ln_g.reshape(1, N), ln_b.reshape(1, N))


def _attn_kernel(q_ref, k_ref, v_ref, o_ref, *, blk, wide):
    i = pl.program_id(2)
    q = q_ref[0, 0]

    def step(carry, off, width, mask):
        m, l, acc = carry
        k = k_ref[0, 0, pl.ds(off, width), :]
        s = lax.dot_general(q, k, (((1,), (1,)), ((), ())), preferred_element_type=F32)
        if mask is not None:
            s = jnp.where(mask, s, NEG_BIG)
        m_new = jnp.maximum(m, jnp.max(s, axis=-1, keepdims=True))
        alpha = jnp.exp2(m - m_new)
        p = jnp.exp2(s - m_new)
        l = alpha * l + jnp.sum(p, axis=-1, keepdims=True)
        v = v_ref[0, 0, pl.ds(off, width), :]
        acc = alpha * acc + jnp.dot(p.astype(BF16), v, preferred_element_type=F32)
        return m_new, l, acc

    carry = (jnp.full((blk, 1), NEG_BIG, F32), jnp.zeros((blk, 1), F32), jnp.zeros((blk, MLA_V_DIM), F32))
    n_wide = i // wide
    carry = lax.fori_loop(
        0, n_wide, lambda j, c: step(c, pl.multiple_of(j * (wide * blk), wide * blk), wide * blk, None), carry)
    carry = lax.fori_loop(
        n_wide * wide, i, lambda j, c: step(c, pl.multiple_of(j * blk, blk), blk, None), carry)
    row = lax.broadcasted_iota(jnp.int32, (blk, blk), 0)
    col = lax.broadcasted_iota(jnp.int32, (blk, blk), 1)
    m, l, acc = step(carry, pl.multiple_of(i * blk, blk), blk, row >= col)
    o_ref[0] = (acc / l).astype(o_ref.dtype)


def _attention(q, k, v, blk=1024, wide=1):
    B, H, S, qk_w = q.shape
    blk = min(blk, S)
    assert S % blk == 0
    need = 2 * (_nbytes((blk, 256), BF16) + _nbytes((S, 256), BF16) + _nbytes((S, 128), BF16)
                + _nbytes((blk, 128), BF16)) + 6 * _nbytes((blk, blk), F32)
    return pl.pallas_call(
        functools.partial(_attn_kernel, blk=blk, wide=wide),
        grid=(B, H, S // blk),
        in_specs=[pl.BlockSpec((1, 1, blk, qk_w), lambda b, h, i: (b, h, i, 0)),
                  pl.BlockSpec((1, 1, S, qk_w), lambda b, h, i: (b, h, 0, 0)),
                  pl.BlockSpec((1, 1, S, MLA_V_DIM), lambda b, h, i: (b, h, 0, 0))],
        out_specs=pl.BlockSpec((1, blk, MLA_V_DIM), lambda b, h, i: (b, i, h)),
        out_shape=jax.ShapeDtypeStruct((B, S, H * MLA_V_DIM), BF16),
        compiler_params=pltpu.CompilerParams(
            dimension_semantics=("parallel", "parallel", "parallel"),
            vmem_limit_bytes=_vmem_limit(need + (8 << 20))),
        name="mla_attention",
    )(q, k, v)


def _gdn_kernel(pq_ref, pk_ref, pv_ref, pz_ref, hq_ref, hk_ref, hv_ref, cq_ref, ck_ref, cv_ref,
                g_ref, b_ref, ng_ref, o_ref, state_ref, *, n_chunks, k_heads):
    C = CHUNK
    rep = GDN_V_HEADS // GDN_K_HEADS
    v_heads = k_heads * rep
    DK, DV = GDN_K_DIM, GDN_V_DIM
    first_block = pl.program_id(2) == 0

    @pl.when(first_block)
    def _():
        state_ref[...] = jnp.zeros_like(state_ref)

    row = lax.broadcasted_iota(jnp.int32, (C, C), 0)
    col = lax.broadcasted_iota(jnp.int32, (C, C), 1)
    causal = row >= col
    strict = row > col
    eye = row == col
    eye_f = eye.astype(F32)
    nt = (((1,), (1,)), ((), ()))
    tn = (((0,), (0,)), ((), ()))

    def dot(a, b):
        return jnp.dot(a, b, preferred_element_type=F32)

    def conv_silu(x_ref, halo_ref, w_ref, c, r0):
        cur = x_ref[0, pl.ds(r0, C), :]
        prev_start = pl.multiple_of(jnp.maximum(c * (C // 8) - 1, 0) * 8, 8)
        prev_rows = x_ref[0, pl.ds(prev_start, 8), :]
        halo = jnp.where(first_block, 0.0, halo_ref[0])
        ext = jnp.concatenate([jnp.where(c == 0, halo, prev_rows), cur], axis=0)
        acc = cur * w_ref[CONV_K - 1:CONV_K, :]
        for j in range(1, CONV_K):
            acc = acc + pltpu.roll(ext, j, axis=0)[8:] * w_ref[CONV_K - 1 - j:CONV_K - j, :]
        return acc * jax.nn.sigmoid(acc)

    def l2n(t):
        return t * lax.rsqrt(jnp.sum(t * t, axis=-1, keepdims=True) + RMS_EPS)

    def chunk(c, carry):
        r0 = pl.multiple_of(c * C, C)
        qa = conv_silu(pq_ref, hq_ref, cq_ref, c, r0)
        ka = conv_silu(pk_ref, hk_ref, ck_ref, c, r0)
        va = conv_silu(pv_ref, hv_ref, cv_ref, c, r0)
        q = [l2n(qa[:, i * DK:(i + 1) * DK]) * (DK ** -0.5) for i in range(k_heads)]
        k = [l2n(ka[:, i * DK:(i + 1) * DK]) for i in range(k_heads)]
        v = [va[:, h * DV:(h + 1) * DV] for h in range(v_heads)]
        k16 = [t.astype(BF16) for t in k]
        qk16 = [jnp.concatenate([q[i], k[i]], axis=0).astype(BF16) for i in range(k_heads)]
        qkk = [lax.dot_general(qk16[i], k16[i], nt, preferred_element_type=F32) for i in range(k_heads)]

        g_row = [g_ref[0, h, pl.ds(c, 1), :] for h in range(v_heads)]
        b_row = [b_ref[0, h, pl.ds(c, 1), :] for h in range(v_heads)]
        g_col = [jnp.sum(jnp.where(eye, t, 0.0), axis=1, keepdims=True) for t in g_row]
        b_col = [jnp.sum(jnp.where(eye, t, 0.0), axis=1, keepdims=True) for t in b_row]
        g_last = [t[:, C - 1:C] for t in g_row]
        decay = [jnp.exp(jnp.where(causal, g_col[h] - g_row[h], NEG_BIG)) for h in range(v_heads)]
        a = [jnp.where(strict, qkk[h // rep][C:] * b_col[h] * decay[h], 0.0) for h in range(v_heads)]
        qkd = [(qkk[h // rep][:C] * decay[h]).astype(BF16) for h in range(v_heads)]

        a16 = [t.astype(BF16) for t in a]
        x = [dot(t, t) for t in a16]
        p = [eye_f - t for t in a]
        for _ in range(int(math.log2(C)) - 2):
            x16 = [t.astype(BF16) for t in x]
            px = [dot(jnp.concatenate([p[h].astype(BF16), x16[h]], axis=0), x16[h]) for h in range(v_heads)]
            p = [p[h] + px[h][:C] for h in range(v_heads)]
            x = [px[h][C:] for h in range(v_heads)]
        p = [p[h] + dot(p[h].astype(BF16), x[h].astype(BF16)) for h in range(v_heads)]

        eg = [jnp.exp(t) for t in g_col]
        rhs = [jnp.concatenate([v[h] * b_col[h], k[h // rep] * (b_col[h] * eg[h])], axis=1).astype(BF16)
               for h in range(v_heads)]
        sol = [dot(p[h].astype(BF16), rhs[h]) for h in range(v_heads)]
        s = [state_ref[h] for h in range(v_heads)]
        wq = [jnp.concatenate([sol[h][:, DV:], q[h // rep] * eg[h]], axis=0).astype(BF16)
              for h in range(v_heads)]
        ws = [dot(wq[h], s[h].astype(BF16)) for h in range(v_heads)]
        u16 = [(sol[h][:, :DV] - ws[h][:C]).astype(BF16) for h in range(v_heads)]
        k_dec = [(k[h // rep] * jnp.exp(g_last[h] - g_col[h])).astype(BF16) for h in range(v_heads)]
        ds = [lax.dot_general(k_dec[h], u16[h], tn, preferred_element_type=F32) for h in range(v_heads)]
        o = [ws[h][C:] + dot(qkd[h], u16[h]) for h in range(v_heads)]
        for h in range(v_heads):
            state_ref[h] = s[h] * jnp.exp(g_last[h]) + ds[h]
            z = pz_ref[0, pl.ds(r0, C), h * DV:(h + 1) * DV]
            on = o[h] * lax.rsqrt(jnp.mean(o[h] * o[h], axis=-1, keepdims=True) + RMS_EPS) * ng_ref[...]
            o_ref[0, pl.ds(r0, C), h * DV:(h + 1) * DV] = (on * (z * jax.nn.sigmoid(z))).astype(o_ref.dtype)
        return carry

    lax.fori_loop(0, n_chunks, chunk, 0)


def _gated_delta_rule(p, conv_w, g_cum, beta, norm_g, n_chunks=8, k_heads=8):
    B, S, _ = p.shape
    n_chunks = min(n_chunks, S // CHUNK)
    sb = n_chunks * CHUNK
    rep = GDN_V_HEADS // GDN_K_HEADS
    v_heads = k_heads * rep
    wq, wv = k_heads * GDN_K_DIM, v_heads * GDN_V_DIM
    assert S % sb == 0 and GDN_K_HEADS % k_heads == 0
    k_off, v_off, z_off = GDN_QK_WIDTH // wq, 2 * GDN_QK_WIDTH // wv, GDN_CONV_DIM // wv
    hb = sb // 8

    def main(width, off):
        return pl.BlockSpec((1, sb, width), lambda b, h, s: (b, s, off + h))

    def halo(width, off):
        return pl.BlockSpec((1, 8, width), lambda b, h, s: (b, jnp.maximum(s * hb - 1, 0), off + h))

    def cw(width, off):
        return pl.BlockSpec((CONV_K, width), lambda b, h, s: (0, off + h))

    blocks = 2 * sb * (2 * wq + 2 * wv) * 4 + 2 * sb * wv * 2
    return pl.pallas_call(
        functools.partial(_gdn_kernel, n_chunks=n_chunks, k_heads=k_heads),
        grid=(B, GDN_K_HEADS // k_heads, S // sb),
        in_specs=[main(wq, 0), main(wq, k_off), main(wv, v_off), main(wv, z_off),
                  halo(wq, 0), halo(wq, k_off), halo(wv, v_off),
                  cw(wq, 0), cw(wq, k_off), cw(wv, v_off),
                  pl.BlockSpec((1, v_heads, n_chunks, CHUNK), lambda b, h, s: (b, h, s, 0)),
                  pl.BlockSpec((1, v_heads, n_chunks, CHUNK), lambda b, h, s: (b, h, s, 0)),
                  pl.BlockSpec((1, GDN_V_DIM), lambda b, h, s: (0, 0))],
        out_specs=pl.BlockSpec((1, sb, wv), lambda b, h, s: (b, s, h)),
        out_shape=jax.ShapeDtypeStruct((B, S, GDN_V_WIDTH), BF16),
        scratch_shapes=[pltpu.VMEM((v_heads, GDN_K_DIM, GDN_V_DIM), F32)],
        compiler_params=pltpu.CompilerParams(
            dimension_semantics=("parallel", "parallel", "arbitrary"),
            vmem_limit_bytes=_vmem_limit(blocks + (12 << 20))),
        name="gated_delta_rule",
    )(p, p, p, p, p, p, p, conv_w, conv_w, conv_w, g_cum, beta, norm_g)


def _expert_kernel(be_ref, nu_ref, x_ref, wgu_ref, wd_ref, *rest):
    o_ref, wgu16_ref, wd16_ref = rest[-3:]
    i = pl.program_id(0)
    used = i < nu_ref[0]
    new_expert = jnp.logical_or(i == 0, be_ref[i] != be_ref[jnp.maximum(i - 1, 0)])

    @pl.when(jnp.logical_and(used, new_expert))
    def _():
        wgu16_ref[...] = wgu_ref[0, 0].astype(BF16)
        wd16_ref[...] = wd_ref[0, 0].astype(BF16)

    @pl.when(used)
    def _():
        x_lo, x_hi = _unpack_halves(x_ref[...])
        half = x_lo.shape[-1]
        gu = (jnp.dot(x_lo.astype(BF16), wgu16_ref[:half, :], preferred_element_type=F32)
              + jnp.dot(x_hi.astype(BF16), wgu16_ref[half:, :], preferred_element_type=F32))
        f = gu.shape[-1] // 2
        gate = gu[:, :f]
        act = gate * jax.nn.sigmoid(gate) * gu[:, f:]
        y = jnp.dot(act.astype(BF16), wd16_ref[...], preferred_element_type=F32)
        if o_ref.dtype == jnp.uint32:
            o_ref[...] = _pack_halves(y)
        else:
            o_ref[...] = y.astype(o_ref.dtype)

    @pl.when(jnp.logical_not(used))
    def _():
        o_ref[...] = jnp.zeros_like(o_ref)


def _grouped_swiglu(x_rows, w_gate_up, w_down, layer, blk_expert, n_used, tm, out_dtype, name,
                    out_prev=None, out_block0=0):
    R = x_rows.shape[0]
    D = w_gate_up.shape[-2]
    F2 = w_gate_up.shape[-1]
    tm = min(tm, R)
    assert R % tm == 0 and x_rows.shape[1] * 2 == D
    d_out = D // 2 if out_dtype == jnp.uint32 else D
    w_elems = D * F2 + (F2 // 2) * D
    need = (2 * w_elems * 4 + w_elems * 2 + 2 * _nbytes((tm, D), BF16) + 2 * _nbytes((tm, D), F32)
            + 2 * _nbytes((tm, F2), F32) + _nbytes((tm, D), F32))
    in_specs = [pl.BlockSpec((tm, D // 2), lambda i, be, nu: (i, 0)),
                pl.BlockSpec((1, 1, D, F2), lambda i, be, nu: (layer, be[i], 0, 0)),
                pl.BlockSpec((1, 1, F2 // 2, D), lambda i, be, nu: (layer, be[i], 0, 0))]
    args = [blk_expert, n_used, x_rows, w_gate_up, w_down]
    aliases = {}
    out_rows = R
    if out_prev is not None:
        in_specs.append(pl.BlockSpec(memory_space=pl.ANY))
        aliases = {len(args): 0}
        args.append(out_prev)
        out_rows = out_prev.shape[0]
    grid_spec = pltpu.PrefetchScalarGridSpec(
        num_scalar_prefetch=2,
        grid=(R // tm,),
        in_specs=in_specs,
        out_specs=pl.BlockSpec((tm, d_out), lambda i, be, nu: (i + out_block0, 0)),
        scratch_shapes=[pltpu.VMEM((D, F2), BF16), pltpu.VMEM((F2 // 2, D), BF16)],
    )
    return pl.pallas_call(
        _expert_kernel,
        grid_spec=grid_spec,
        out_shape=jax.ShapeDtypeStruct((out_rows, d_out), out_dtype),
        input_output_aliases=aliases,
        compiler_params=pltpu.CompilerParams(
            dimension_semantics=("arbitrary",),
            vmem_limit_bytes=_vmem_limit(need + (4 << 20))),
        name=name,
    )(*args)


def _combine_kernel(x_ref, sh_ref, y_ref, gate_ref, g_ref, b_ref, o_ref, o16_ref):
    half = x_ref.shape[-1] // 2
    acc = ALPHA * x_ref[...] + sh_ref[...]
    acc_lo, acc_hi = acc[:, :half], acc[:, half:]
    gate = gate_ref[...]
    for k in range(TOP_K):
        y_lo, y_hi = _unpack_halves(y_ref[k])
        acc_lo = acc_lo + gate[:, k:k + 1] * y_lo
        acc_hi = acc_hi + gate[:, k:k + 1] * y_hi
    out = _layer_norm_rows(jnp.concatenate([acc_lo, acc_hi], axis=1), g_ref[...], b_ref[...])
    o_ref[...] = out
    o16_ref[...] = out.astype(BF16)


def _moe_combine(x, shared, y_tok, gate, ln_g, ln_b, tm=256):
    T, D = x.shape
    tm = min(tm, T)
    assert T % tm == 0
    need = 2 * (3 * _nbytes((tm, D), F32) + _nbytes((tm, D), BF16) + _nbytes((tm, TOP_K * D), BF16)) \
        + 3 * _nbytes((tm, D), F32)
    return pl.pallas_call(
        _combine_kernel,
        grid=(T // tm,),
        in_specs=[pl.BlockSpec((tm, D), lambda i: (i, 0)),
                  pl.BlockSpec((tm, D), lambda i: (i, 0)),
                  pl.BlockSpec((TOP_K, tm, D // 2), lambda i: (0, i, 0)),
                  pl.BlockSpec((tm, TOP_K), lambda i: (i, 0)),
                  pl.BlockSpec((1, D), lambda i: (0, 0)),
                  pl.BlockSpec((1, D), lambda i: (0, 0))],
        out_specs=[pl.BlockSpec((tm, D), lambda i: (i, 0)),
                   pl.BlockSpec((tm, D), lambda i: (i, 0))],
        out_shape=[jax.ShapeDtypeStruct((T, D), F32), jax.ShapeDtypeStruct((T, D), BF16)],
        compiler_params=pltpu.CompilerParams(
            dimension_semantics=("parallel",),
            vmem_limit_bytes=_vmem_limit(need + (4 << 20))),
        name="moe_combine",
    )(x, shared, y_tok, gate, ln_g.reshape(1, D), ln_b.reshape(1, D))


MLA_QK_PAD = 2 * MLA_NOPE_DIM
MLA_IN_PAD = MLA_Q_RANK + MLA_KV_RANK + 2 * MLA_NOPE_DIM


def _mla_proj_kernel(x_ref, win_ref, wq_ref, wkv_ref, qg_ref, kvg_ref, csq_ref, ck_ref, sk_ref,
                     q_ref, k_ref, v_ref):
    NP = MLA_NOPE_DIM
    c = jnp.dot(x_ref[...], win_ref[...], preferred_element_type=F32)

    def rms(t, g):
        return (t * lax.rsqrt(jnp.mean(t * t, axis=-1, keepdims=True) + RMS_EPS) * g).astype(BF16)

    cq = rms(c[:, :MLA_Q_RANK], qg_ref[...])
    ckv = rms(c[:, MLA_Q_RANK:MLA_Q_RANK + MLA_KV_RANK], kvg_ref[...])
    o = MLA_Q_RANK + MLA_KV_RANK
    k_rope = (c[:, o:o + NP] * ck_ref[...] + c[:, o + NP:o + 2 * NP] * sk_ref[...]).astype(BF16)
    low = lax.broadcasted_iota(jnp.int32, (x_ref.shape[0], NP), 1) < MLA_ROPE_DIM
    csq = csq_ref[...]
    for h in range(MLA_HEADS):
        qh = jnp.dot(cq, wq_ref[:, h * 2 * NP:(h + 1) * 2 * NP], preferred_element_type=F32)
        u = qh[:, NP:] * csq
        q_rope = jnp.where(low, u + pltpu.roll(u, MLA_ROPE_DIM, axis=1), 0.0)
        q_ref[0, h, :, :NP] = (qh[:, :NP] * MLA_Q_SCALE).astype(BF16)
        q_ref[0, h, :, NP:] = (q_rope * MLA_Q_SCALE).astype(BF16)
        kvh = jnp.dot(ckv, wkv_ref[:, h * 2 * NP:(h + 1) * 2 * NP], preferred_element_type=F32)
        k_ref[0, h, :, :NP] = kvh[:, :NP].astype(BF16)
        k_ref[0, h, :, NP:] = k_rope
        v_ref[0, h] = kvh[:, NP:].astype(BF16)


def _mla_proj(x16, w_in_ext, w_q_ext, w_kvb, q_norm_g, kv_norm_g, csq, ck, sk, batch, tm=256):
    T, D = x16.shape
    S = T // batch
    H = MLA_HEADS
    tm = min(tm, S)
    assert S % tm == 0
    nsb = S // tm
    w_bytes = (w_in_ext.size + w_q_ext.size + w_kvb.size) * 2
    out_bytes = _nbytes((H, tm, 2 * MLA_QK_PAD + MLA_V_DIM), BF16)
    need = w_bytes + 2 * _nbytes((tm, D), BF16) + 2 * out_bytes + 4 * _nbytes((tm, MLA_IN_PAD), F32)

    def const(shape):
        return pl.BlockSpec(shape, lambda i: (0, 0), pipeline_mode=pl.Buffered(1))

    def rows(width):
        return pl.BlockSpec((tm, width), lambda i: (i, 0))

    def heads(width):
        return pl.BlockSpec((1, H, tm, width), lambda i: (i // nsb, 0, i % nsb, 0))

    return pl.pallas_call(
        _mla_proj_kernel,
        grid=(T // tm,),
        in_specs=[rows(D), const(w_in_ext.shape), const(w_q_ext.shape), const(w_kvb.shape),
                  const((1, MLA_Q_RANK)), const((1, MLA_KV_RANK)),
                  rows(MLA_NOPE_DIM), rows(MLA_NOPE_DIM), rows(MLA_NOPE_DIM)],
        out_specs=[heads(MLA_QK_PAD), heads(MLA_QK_PAD), heads(MLA_V_DIM)],
        out_shape=[jax.ShapeDtypeStruct((batch, H, S, MLA_QK_PAD), BF16),
                   jax.ShapeDtypeStruct((batch, H, S, MLA_QK_PAD), BF16),
                   jax.ShapeDtypeStruct((batch, H, S, MLA_V_DIM), BF16)],
        compiler_params=pltpu.CompilerParams(
            dimension_semantics=("parallel",),
            vmem_limit_bytes=_vmem_limit(need + (6 << 20))),
        name="mla_proj",
    )(x16, w_in_ext, w_q_ext, w_kvb, q_norm_g.reshape(1, -1), kv_norm_g.reshape(1, -1), csq, ck, sk)


def _router_kernel(x_ref, wt_ref, bias_ref, e_ref, gate_ref, rank_ref, cnt_ref, carry_ref):
    E, G, K = N_EXPERTS, N_GROUPS, TOP_K
    P = E // G
    tm = x_ref.shape[0]

    @pl.when(pl.program_id(0) == 0)
    def _():
        carry_ref[...] = jnp.zeros_like(carry_ref)

    logits = lax.dot_general(wt_ref[...], x_ref[...], (((1,), (1,)), ((), ())),
                             precision=lax.Precision.HIGHEST, preferred_element_type=F32)
    scores = jax.nn.sigmoid(logits)
    biased = scores + bias_ref[...]

    def preceding(v):
        n = v.shape[0]
        idx = lax.broadcasted_iota(jnp.int32, (n, 1), 0)
        cnt = jnp.zeros(v.shape, F32)
        for j in range(n):
            row = v[j:j + 1, :]
            tie = jnp.where(idx > j, 1.0, 0.0)
            cnt = cnt + jnp.where(row > v, 1.0, jnp.where(row == v, tie, 0.0))
        return cnt

    blocks = [biased[g * P:(g + 1) * P, :] for g in range(G)]
    group_score = jnp.concatenate(
        [jnp.sum(jnp.where(preceding(b) < 2.0, b, 0.0), axis=0, keepdims=True) for b in blocks], axis=0)
    group_sel = preceding(group_score) < float(TOPK_GROUPS)
    masked = jnp.concatenate(
        [jnp.where(group_sel[g:g + 1, :], blocks[g], -jnp.inf) for g in range(G)], axis=0)
    sel = preceding(masked) < float(K)
    sel_f = jnp.where(sel, 1.0, 0.0)
    gate = jnp.where(sel, scores, 0.0)
    gate = gate / (jnp.sum(gate, axis=0, keepdims=True) + 1e-20) * ROUTED_SCALE

    t_row = lax.broadcasted_iota(jnp.int32, (tm, tm), 0)
    t_col = lax.broadcasted_iota(jnp.int32, (tm, tm), 1)
    before_t = jnp.where(t_row < t_col, 1.0, 0.0).astype(BF16)
    sel16 = sel_f.astype(BF16)
    rank = jnp.dot(sel16, before_t, preferred_element_type=F32) + carry_ref[...]
    carry_ref[...] = carry_ref[...] + jnp.sum(sel_f, axis=1, keepdims=True)
    e_row = lax.broadcasted_iota(jnp.int32, (E, E), 0)
    e_col = lax.broadcasted_iota(jnp.int32, (E, E), 1)
    before_e = jnp.where(e_col < e_row, 1.0, 0.0).astype(BF16)
    slot = jnp.where(sel, jnp.dot(before_e, sel16, preferred_element_type=F32), -1.0)
    e_idx = lax.broadcasted_iota(jnp.int32, (E, 1), 0).astype(F32)

    def pick(k, t):
        return jnp.sum(jnp.where(slot == float(k), t, 0.0), axis=0, keepdims=True)

    e_ref[...] = jnp.concatenate([pick(k, e_idx) for k in range(K)], axis=0).astype(jnp.int32)
    rank_ref[...] = jnp.concatenate([pick(k, rank) for k in range(K)], axis=0).astype(jnp.int32)
    gate_ref[...] = jnp.concatenate([pick(k, gate) for k in range(K)], axis=0)
    cnt_ref[...] = jnp.broadcast_to(carry_ref[...], cnt_ref.shape).astype(jnp.int32)


def _router(x, router_wt, router_bias, tm=512):
    T, D = x.shape
    E = router_wt.shape[0]
    tm = min(tm, T)
    assert T % tm == 0
    kt = pl.BlockSpec((TOP_K, tm), lambda i: (0, i))
    top_e, gate, rank, cnt = pl.pallas_call(
        _router_kernel,
        grid=(T // tm,),
        in_specs=[pl.BlockSpec((tm, D), lambda i: (i, 0)),
                  pl.BlockSpec((E, D), lambda i: (0, 0)),
                  pl.BlockSpec((E, 1), lambda i: (0, 0))],
        out_specs=[kt, kt, kt, pl.BlockSpec((E, 128), lambda i: (0, 0))],
        out_shape=[jax.ShapeDtypeStruct((TOP_K, T), jnp.int32), jax.ShapeDtypeStruct((TOP_K, T), F32),
                   jax.ShapeDtypeStruct((TOP_K, T), jnp.int32), jax.ShapeDtypeStruct((E, 128), jnp.int32)],
        scratch_shapes=[pltpu.VMEM((E, 1), F32)],
        compiler_params=pltpu.CompilerParams(
            dimension_semantics=("arbitrary",),
            vmem_limit_bytes=_vmem_limit(2 * _nbytes((tm, D), F32) + (16 << 20))),
        name="moe_router",
    )(x, router_wt, router_bias.reshape(E, 1))
    return top_e, gate, rank, cnt[:, 0]


def _mla_mixer(x, x16, positions, w_in, q_norm_g, kv_norm_g, w_qb, w_kvb, w_o, ln_g, ln_b):
    B, S = positions.shape
    T, D = x.shape
    H = MLA_HEADS
    half = MLA_ROPE_DIM // 2

    def rot(w):
        return jnp.concatenate([-w[..., half:], w[..., :half]], axis=-1)

    lat = MLA_Q_RANK + MLA_KV_RANK
    w_kr = w_in[:, lat:]
    zpad = jnp.zeros((D, MLA_NOPE_DIM - MLA_ROPE_DIM), w_in.dtype)
    w_in_ext = jnp.concatenate([w_in[:, :lat], w_kr, zpad, rot(w_kr), zpad], axis=1).astype(BF16)
    w_q3 = w_qb.reshape(MLA_Q_RANK, H, MLA_QK_DIM)
    w_q_rope = w_q3[..., MLA_NOPE_DIM:]
    w_q_ext = jnp.concatenate([w_q3[..., :MLA_NOPE_DIM], w_q_rope, rot(w_q_rope)], axis=-1)
    w_q_ext = w_q_ext.reshape(MLA_Q_RANK, H * MLA_QK_PAD).astype(BF16)
    inv_freq = 1.0 / (ROPE_THETA ** (jnp.arange(0, MLA_ROPE_DIM, 2, dtype=F32) / MLA_ROPE_DIM))
    ang = positions.astype(F32).reshape(T, 1) * inv_freq
    cos, sin = jnp.cos(ang), jnp.sin(ang)
    zero = jnp.zeros_like(cos)
    csq = jnp.concatenate([cos, cos, sin, sin], axis=1)
    ck = jnp.concatenate([cos, cos, zero, zero], axis=1)
    sk = jnp.concatenate([sin, sin, zero, zero], axis=1)
    q_cat, k_cat, v = _mla_proj(x16, w_in_ext, w_q_ext, w_kvb.astype(BF16), q_norm_g, kv_norm_g,
                                csq, ck, sk, B)
    o = _attention(q_cat, k_cat, v)
    return _matmul_ln(o.reshape(T, H * MLA_V_DIM), w_o.astype(BF16), x, ln_g, ln_b, 512, "mla_out_ln")


def _gdn_mixer(x, x16, batch, w_in, conv_w, a_log, dt_bias, norm_g, w_out, ln_g, ln_b):
    T, D = x.shape
    B, S = batch, T // batch
    n_main = GDN_CONV_DIM + GDN_V_WIDTH
    p = _matmul(x16, w_in[:, :n_main].astype(BF16), F32, name="gdn_in")
    ba = _matmul(x16, w_in[:, n_main:].astype(BF16), F32, name="gdn_in_ba")
    b = ba[:, :GDN_V_HEADS].reshape(B, S, GDN_V_HEADS)
    a = ba[:, GDN_V_HEADS:].reshape(B, S, GDN_V_HEADS)
    beta = jax.nn.sigmoid(b)
    g = -jnp.exp(a_log) * jax.nn.softplus(a + dt_bias)
    N = S // CHUNK
    g_cum = jnp.cumsum(g.reshape(B, N, CHUNK, GDN_V_HEADS), axis=2)
    g_cum = jnp.transpose(g_cum, (0, 3, 1, 2))
    beta = jnp.transpose(beta.reshape(B, N, CHUNK, GDN_V_HEADS), (0, 3, 1, 2))
    o = _gated_delta_rule(p.reshape(B, S, n_main), conv_w, g_cum, beta, norm_g.reshape(1, GDN_V_DIM))
    return _matmul_ln(o.reshape(T, GDN_V_WIDTH), w_out.astype(BF16), x, ln_g, ln_b, 256, "gdn_out_ln")


def _moe_ffn(x, xpk, layer, router_w, router_bias, w_gate_up, w_down, shared_gate_up, shared_down, ln_g, ln_b):
    T, D = x.shape
    E, K = N_EXPERTS, TOP_K
    top_e, gate, rank, counts = _router(x, router_w.T, router_bias)

    zero = jnp.zeros((1,), jnp.int32)
    tm_sh = min(512, T)
    shared = _grouped_swiglu(xpk, shared_gate_up, shared_down, layer,
                             jnp.zeros((T // tm_sh,), jnp.int32), zero + T // tm_sh, tm_sh, F32,
                             "shared_expert")

    n_assign = T * K
    n_blocks = -(-n_assign // MOE_BLOCK) + E
    starts = jnp.cumsum(counts) - counts
    padded = (counts + MOE_BLOCK - 1) // MOE_BLOCK * MOE_BLOCK
    pad_ends = jnp.cumsum(padded)
    pad_starts = pad_ends - padded
    e_ids = jnp.arange(E, dtype=jnp.int32)
    pos = jnp.sum(jnp.where(top_e[..., None] == e_ids, pad_starts, 0), axis=-1) + rank
    blk_first = jnp.arange(n_blocks, dtype=jnp.int32) * MOE_BLOCK
    blk_expert = jnp.minimum(jnp.sum((pad_ends[None, :] <= blk_first[:, None]).astype(jnp.int32), axis=1),
                             E - 1)
    n_used = (pad_ends[-1:] // MOE_BLOCK).astype(jnp.int32)
    tok = jnp.arange(T, dtype=jnp.int32)
    order = jnp.sort((top_e * T + tok[None, :]).reshape(n_assign)) % T
    row_j = (blk_first - pad_starts[blk_expert])[:, None] + jnp.arange(MOE_BLOCK, dtype=jnp.int32)[None, :]
    row_src = jnp.clip(starts[blk_expert][:, None] + row_j, 0, n_assign - 1)
    pad_tok = (blk_first[:, None] + jnp.arange(MOE_BLOCK, dtype=jnp.int32)[None, :]) % T
    row_tok = jnp.where(row_j < counts[blk_expert][:, None], order[row_src], pad_tok)

    n_chunks = MOE_GATHER_CHUNKS if n_blocks % MOE_GATHER_CHUNKS == 0 else 1
    cb = n_blocks // n_chunks
    y_rows = jnp.zeros((n_blocks * MOE_BLOCK, D // 2), jnp.uint32)
    for c in range(n_chunks):
        x_rows = xpk.at[row_tok[c * cb:(c + 1) * cb].reshape(-1)].get(mode="promise_in_bounds")
        y_rows = _grouped_swiglu(x_rows, w_gate_up, w_down, layer, blk_expert[c * cb:(c + 1) * cb],
                                 jnp.clip(n_used - c * cb, 0, cb), MOE_BLOCK, jnp.uint32, "routed_experts",
                                 out_prev=y_rows, out_block0=c * cb)
    y_tok = y_rows.at[pos.reshape(n_assign)].get(mode="promise_in_bounds").reshape(K, T, D // 2)
    return _moe_combine(x, shared, y_tok, gate.T, ln_g, ln_b)


def kernel(x, positions, mla_w_in, mla_q_norm, mla_kv_norm, mla_w_qb, mla_w_kvb, mla_w_o, gdn_w_in, gdn_conv_w, gdn_a_log, gdn_dt_bias, gdn_norm, gdn_w_out, ln_mix_g, ln_mix_b, ln_ffn_g, ln_ffn_b, moe_router, moe_router_bias, moe_w_gate_up, moe_w_down, moe_shared_gate_up, moe_shared_down):
    B, S, D = x.shape
    x = x.reshape(B * S, D)
    x16 = x.astype(BF16)
    for i in range(DEPTH):
        j = i // 2
        if i % 2 == 0:
            x, xpk = _mla_mixer(x, x16, positions, mla_w_in[j], mla_q_norm[j], mla_kv_norm[j],
                                mla_w_qb[j], mla_w_kvb[j], mla_w_o[j], ln_mix_g[i], ln_mix_b[i])
        else:
            x, xpk = _gdn_mixer(x, x16, B, gdn_w_in[j], gdn_conv_w[j], gdn_a_log[j], gdn_dt_bias[j],
                                gdn_norm[j], gdn_w_out[j], ln_mix_g[i], ln_mix_b[i])
        x, x16 = _moe_ffn(x, xpk, i, moe_router[i], moe_router_bias[i], moe_w_gate_up, moe_w_down,
                          moe_shared_gate_up[:, None], moe_shared_down[:, None], ln_ffn_g[i], ln_ffn_b[i])
    return x.reshape(B, S, D)
```

```python
import functools
import math

import jax
import jax.numpy as jnp
from jax import lax
from jax.experimental import pallas as pl
from jax.experimental.pallas import tpu as pltpu

F32 = jnp.float32
BF16 = jnp.bfloat16

DEPTH = 2
MLA_HEADS = 16
MLA_Q_RANK = 512
MLA_KV_RANK = 512
MLA_NOPE_DIM = 128
MLA_ROPE_DIM = 64
MLA_V_DIM = 128
MLA_QK_DIM = MLA_NOPE_DIM + MLA_ROPE_DIM
MLA_SCALE = MLA_QK_DIM ** -0.5
MLA_Q_SCALE = MLA_SCALE * math.log2(math.e)
ROPE_THETA = 10000.0

GDN_K_HEADS = 16
GDN_V_HEADS = 32
GDN_K_DIM = 128
GDN_V_DIM = 128
CONV_K = 4
CHUNK = 64
GDN_QK_WIDTH = GDN_K_HEADS * GDN_K_DIM
GDN_V_WIDTH = GDN_V_HEADS * GDN_V_DIM
GDN_CONV_DIM = 2 * GDN_QK_WIDTH + GDN_V_WIDTH

N_EXPERTS = 64
TOP_K = 8
N_GROUPS = 8
TOPK_GROUPS = 4
EXPERT_DIM = 512
ROUTED_SCALE = 2.5
MOE_BLOCK = 256
MOE_GATHER_CHUNKS = 4

ALPHA = (2 * DEPTH) ** 0.25
LN_EPS = 1e-5
RMS_EPS = 1e-6

V7X_VMEM_BYTES = 64 * 1024 * 1024
VMEM_LIMIT_CAP = 56 * 1024 * 1024
NEG_BIG = -1e30


def _vmem_limit(nbytes):
    return int(min(max(nbytes, 16 * 1024 * 1024), VMEM_LIMIT_CAP))


def _nbytes(shape, dtype):
    return math.prod(shape) * jnp.dtype(dtype).itemsize


def _layer_norm_rows(y, g, b):
    mu = jnp.mean(y, axis=-1, keepdims=True)
    yc = y - mu
    var = jnp.mean(yc * yc, axis=-1, keepdims=True)
    return yc * lax.rsqrt(var + LN_EPS) * g + b


def _pack_halves(x):
    n = x.shape[-1] // 2
    lo = pltpu.bitcast(x[:, :n].astype(BF16).astype(F32), jnp.uint32)
    hi = pltpu.bitcast(x[:, n:].astype(BF16).astype(F32), jnp.uint32)
    return hi | (lo >> 16)


def _unpack_halves(w):
    lo = pltpu.bitcast(w << 16, F32)
    hi = pltpu.bitcast(w & jnp.uint32(0xFFFF0000), F32)
    return lo, hi


def _mm_kernel(x_ref, w_ref, o_ref):
    o_ref[...] = jnp.dot(x_ref[...], w_ref[...], preferred_element_type=F32).astype(o_ref.dtype)


def _matmul(x, w, out_dtype, tm=1024, tn=1024, name="matmul"):
    M, K = x.shape
    N = w.shape[1]
    tm = min(tm, M)
    tn = min(tn, N)
    assert M % tm == 0 and N % tn == 0, (M, N, tm, tn)
    need = 2 * (_nbytes((tm, K), x.dtype) + _nbytes((K, tn), w.dtype) + _nbytes((tm, tn), out_dtype))
    need += _nbytes((tm, tn), F32)
    return pl.pallas_call(
        _mm_kernel,
        grid=(N // tn, M // tm),
        in_specs=[pl.BlockSpec((tm, K), lambda j, i: (i, 0)),
                  pl.BlockSpec((K, tn), lambda j, i: (0, j))],
        out_specs=pl.BlockSpec((tm, tn), lambda j, i: (i, j)),
        out_shape=jax.ShapeDtypeStruct((M, N), out_dtype),
        compiler_params=pltpu.CompilerParams(
            dimension_semantics=("parallel", "parallel"),
            vmem_limit_bytes=_vmem_limit(need + (4 << 20))),
        name=name,
    )(x, w)


def _mm_ln_kernel(x_ref, w_ref, r_ref, g_ref, b_ref, o_ref, opk_ref):
    y = ALPHA * r_ref[...] + jnp.dot(x_ref[...], w_ref[...], preferred_element_type=F32)
    out = _layer_norm_rows(y, g_ref[...], b_ref[...])
    o_ref[...] = out
    opk_ref[...] = _pack_halves(out)


def _matmul_ln(x, w, resid, ln_g, ln_b, tm, name):
    M, K = x.shape
    N = w.shape[1]
    tm = min(tm, M)
    assert M % tm == 0
    need = (_nbytes((K, N), BF16) + 2 * _nbytes((tm, K), BF16) + 4 * _nbytes((tm, N), F32)
            + 2 * _nbytes((tm, N), BF16) + 2 * _nbytes((tm, N), F32))
    return pl.pallas_call(
        _mm_ln_kernel,
        grid=(M // tm,),
        in_specs=[pl.BlockSpec((tm, K), lambda i: (i, 0)),
                  pl.BlockSpec((K, N), lambda i: (0, 0), pipeline_mode=pl.Buffered(1)),
                  pl.BlockSpec((tm, N), lambda i: (i, 0)),
                  pl.BlockSpec((1, N), lambda i: (0, 0)),
                  pl.BlockSpec((1, N), lambda i: (0, 0))],
        out_specs=[pl.BlockSpec((tm, N), lambda i: (i, 0)),
                   pl.BlockSpec((tm, N // 2), lambda i: (i, 0))],
        out_shape=[jax.ShapeDtypeStruct((M, N), F32), jax.ShapeDtypeStruct((M, N // 2), jnp.uint32)],
        compiler_params=pltpu.CompilerParams(
            dimension_semantics=("parallel",),
            vmem_limit_bytes=_vmem_limit(need + (4 << 20))),
        name=name,
    )(x, w, resid, ln_g.reshape(1, N), ln_b.reshape(1, N))


def _attn_kernel(q_ref, k_ref, v_ref, o_ref, *, blk, wide):
    i = pl.program_id(2)
    q = q_ref[0, 0]

    def step(carry, off, width, mask):
        m, l, acc = carry
        k = k_ref[0, 0, pl.ds(off, width), :]
        s = lax.dot_general(q, k, (((1,), (1,)), ((), ())), preferred_element_type=F32)
        if mask is not None:
            s = jnp.where(mask, s, NEG_BIG)
        m_new = jnp.maximum(m, jnp.max(s, axis=-1, keepdims=True))
        alpha = jnp.exp2(m - m_new)
        p = jnp.exp2(s - m_new)
        l = alpha * l + jnp.sum(p, axis=-1, keepdims=True)
        v = v_ref[0, 0, pl.ds(off, width), :]
        acc = alpha * acc + jnp.dot(p.astype(BF16), v, preferred_element_type=F32)
        return m_new, l, acc

    carry = (jnp.full((blk, 1), NEG_BIG, F32), jnp.zeros((blk, 1), F32), jnp.zeros((blk, MLA_V_DIM), F32))
    n_wide = i // wide
    carry = lax.fori_loop(
        0, n_wide, lambda j, c: step(c, pl.multiple_of(j * (wide * blk), wide * blk), wide * blk, None), carry)
    carry = lax.fori_loop(
        n_wide * wide, i, lambda j, c: step(c, pl.multiple_of(j * blk, blk), blk, None), carry)
    row = lax.broadcasted_iota(jnp.int32, (blk, blk), 0)
    col = lax.broadcasted_iota(jnp.int32, (blk, blk), 1)
    m, l, acc = step(carry, pl.multiple_of(i * blk, blk), blk, row >= col)
    o_ref[0] = (acc / l).astype(o_ref.dtype)


def _attention(q, k, v, blk=1024, wide=1):
    B, H, S, qk_w = q.shape
    blk = min(blk, S)
    assert S % blk == 0
    need = 2 * (_nbytes((blk, 256), BF16) + _nbytes((S, 256), BF16) + _nbytes((S, 128), BF16)
                + _nbytes((blk, 128), BF16)) + 6 * _nbytes((blk, blk), F32)
    return pl.pallas_call(
        functools.partial(_attn_kernel, blk=blk, wide=wide),
        grid=(B, H, S // blk),
        in_specs=[pl.BlockSpec((1, 1, blk, qk_w), lambda b, h, i: (b, h, i, 0)),
                  pl.BlockSpec((1, 1, S, qk_w), lambda b, h, i: (b, h, 0, 0)),
                  pl.BlockSpec((1, 1, S, MLA_V_DIM), lambda b, h, i: (b, h, 0, 0))],
        out_specs=pl.BlockSpec((1, blk, MLA_V_DIM), lambda b, h, i: (b, i, h)),
        out_shape=jax.ShapeDtypeStruct((B, S, H * MLA_V_DIM), BF16),
        compiler_params=pltpu.CompilerParams(
            dimension_semantics=("parallel", "parallel", "parallel"),
            vmem_limit_bytes=_vmem_limit(need + (8 << 20))),
        name="mla_attention",
    )(q, k, v)


def _gdn_kernel(pq_ref, pk_ref, pv_ref, pz_ref, hq_ref, hk_ref, hv_ref, cq_ref, ck_ref, cv_ref,
                g_ref, b_ref, ng_ref, o_ref, state_ref, *, n_chunks, k_heads):
    C = CHUNK
    rep = GDN_V_HEADS // GDN_K_HEADS
    v_heads = k_heads * rep
    DK, DV = GDN_K_DIM, GDN_V_DIM
    first_block = pl.program_id(2) == 0

    @pl.when(first_block)
    def _():
        state_ref[...] = jnp.zeros_like(state_ref)

    row = lax.broadcasted_iota(jnp.int32, (C, C), 0)
    col = lax.broadcasted_iota(jnp.int32, (C, C), 1)
    causal = row >= col
    strict = row > col
    eye = row == col
    eye_f = eye.astype(F32)
    nt = (((1,), (1,)), ((), ()))
    tn = (((0,), (0,)), ((), ()))

    def dot(a, b):
        return jnp.dot(a, b, preferred_element_type=F32)

    def conv_silu(x_ref, halo_ref, w_ref, c, r0):
        cur = x_ref[0, pl.ds(r0, C), :]
        prev_start = pl.multiple_of(jnp.maximum(c * (C // 8) - 1, 0) * 8, 8)
        prev_rows = x_ref[0, pl.ds(prev_start, 8), :]
        halo = jnp.where(first_block, 0.0, halo_ref[0])
        ext = jnp.concatenate([jnp.where(c == 0, halo, prev_rows), cur], axis=0)
        acc = cur * w_ref[CONV_K - 1:CONV_K, :]
        for j in range(1, CONV_K):
            acc = acc + pltpu.roll(ext, j, axis=0)[8:] * w_ref[CONV_K - 1 - j:CONV_K - j, :]
        return acc * jax.nn.sigmoid(acc)

    def l2n(t):
        return t * lax.rsqrt(jnp.sum(t * t, axis=-1, keepdims=True) + RMS_EPS)

    def chunk(c, carry):
        r0 = pl.multiple_of(c * C, C)
        qa = conv_silu(pq_ref, hq_ref, cq_ref, c, r0)
        ka = conv_silu(pk_ref, hk_ref, ck_ref, c, r0)
        va = conv_silu(pv_ref, hv_ref, cv_ref, c, r0)
        q = [l2n(qa[:, i * DK:(i + 1) * DK]) * (DK ** -0.5) for i in range(k_heads)]
        k = [l2n(ka[:, i * DK:(i + 1) * DK]) for i in range(k_heads)]
        v = [va[:, h * DV:(h + 1) * DV] for h in range(v_heads)]
        k16 = [t.astype(BF16) for t in k]
        qk16 = [jnp.concatenate([q[i], k[i]], axis=0).astype(BF16) for i in range(k_heads)]
        qkk = [lax.dot_general(qk16[i], k16[i], nt, preferred_element_type=F32) for i in range(k_heads)]

        g_row = [g_ref[0, h, pl.ds(c, 1), :] for h in range(v_heads)]
        b_row = [b_ref[0, h, pl.ds(c, 1), :] for h in range(v_heads)]
        g_col = [jnp.sum(jnp.where(eye, t, 0.0), axis=1, keepdims=True) for t in g_row]
        b_col = [jnp.sum(jnp.where(eye, t, 0.0), axis=1, keepdims=True) for t in b_row]
        g_last = [t[:, C - 1:C] for t in g_row]
        decay = [jnp.exp(jnp.where(causal, g_col[h] - g_row[h], NEG_BIG)) for h in range(v_heads)]
        a = [jnp.where(strict, qkk[h // rep][C:] * b_col[h] * decay[h], 0.0) for h in range(v_heads)]
        qkd = [(qkk[h // rep][:C] * decay[h]).astype(BF16) for h in range(v_heads)]

        a16 = [t.astype(BF16) for t in a]
        x = [dot(t, t) for t in a16]
        p = [eye_f - t for t in a]
        for _ in range(int(math.log2(C)) - 2):
            x16 = [t.astype(BF16) for t in x]
            px = [dot(jnp.concatenate([p[h].astype(BF16), x16[h]], axis=0), x16[h]) for h in range(v_heads)]
            p = [p[h] + px[h][:C] for h in range(v_heads)]
            x = [px[h][C:] for h in range(v_heads)]
        p = [p[h] + dot(p[h].astype(BF16), x[h].astype(BF16)) for h in range(v_heads)]

        eg = [jnp.exp(t) for t in g_col]
        rhs = [jnp.concatenate([v[h] * b_col[h], k[h // rep] * (b_col[h] * eg[h])], axis=1).astype(BF16)
               for h in range(v_heads)]
        sol = [dot(p[h].astype(BF16), rhs[h]) for h in range(v_heads)]
        s = [state_ref[h] for h in range(v_heads)]
        wq = [jnp.concatenate([sol[h][:, DV:], q[h // rep] * eg[h]], axis=0).astype(BF16)
              for h in range(v_heads)]
        ws = [dot(wq[h], s[h].astype(BF16)) for h in range(v_heads)]
        u16 = [(sol[h][:, :DV] - ws[h][:C]).astype(BF16) for h in range(v_heads)]
        k_dec = [(k[h // rep] * jnp.exp(g_last[h] - g_col[h])).astype(BF16) for h in range(v_heads)]
        ds = [lax.dot_general(k_dec[h], u16[h], tn, preferred_element_type=F32) for h in range(v_heads)]
        o = [ws[h][C:] + dot(qkd[h], u16[h]) for h in range(v_heads)]
        for h in range(v_heads):
            state_ref[h] = s[h] * jnp.exp(g_last[h]) + ds[h]
            z = pz_ref[0, pl.ds(r0, C), h * DV:(h + 1) * DV]
            on = o[h] * lax.rsqrt(jnp.mean(o[h] * o[h], axis=-1, keepdims=True) + RMS_EPS) * ng_ref[...]
            o_ref[0, pl.ds(r0, C), h * DV:(h + 1) * DV] = (on * (z * jax.nn.sigmoid(z))).astype(o_ref.dtype)
        return carry

    lax.fori_loop(0, n_chunks, chunk, 0)


def _gated_delta_rule(p, conv_w, g_cum, beta, norm_g, n_chunks=8, k_heads=8):
    B, S, _ = p.shape
    n_chunks = min(n_chunks, S // CHUNK)
    sb = n_chunks * CHUNK
    rep = GDN_V_HEADS // GDN_K_HEADS
    v_heads = k_heads * rep
    wq, wv = k_heads * GDN_K_DIM, v_heads * GDN_V_DIM
    assert S % sb == 0 and GDN_K_HEADS % k_heads == 0
    k_off, v_off, z_off = GDN_QK_WIDTH // wq, 2 * GDN_QK_WIDTH // wv, GDN_CONV_DIM // wv
    hb = sb // 8

    def main(width, off):
        return pl.BlockSpec((1, sb, width), lambda b, h, s: (b, s, off + h))

    def halo(width, off):
        return pl.BlockSpec((1, 8, width), lambda b, h, s: (b, jnp.maximum(s * hb - 1, 0), off + h))

    def cw(width, off):
        return pl.BlockSpec((CONV_K, width), lambda b, h, s: (0, off + h))

    blocks = 2 * sb * (2 * wq + 2 * wv) * 4 + 2 * sb * wv * 2
    return pl.pallas_call(
        functools.partial(_gdn_kernel, n_chunks=n_chunks, k_heads=k_heads),
        grid=(B, GDN_K_HEADS // k_heads, S // sb),
        in_specs=[main(wq, 0), main(wq, k_off), main(wv, v_off), main(wv, z_off),
                  halo(wq, 0), halo(wq, k_off), halo(wv, v_off),
                  cw(wq, 0), cw(wq, k_off), cw(wv, v_off),
                  pl.BlockSpec((1, v_heads, n_chunks, CHUNK), lambda b, h, s: (b, h, s, 0)),
                  pl.BlockSpec((1, v_heads, n_chunks, CHUNK), lambda b, h, s: (b, h, s, 0)),
                  pl.BlockSpec((1, GDN_V_DIM), lambda b, h, s: (0, 0))],
        out_specs=pl.BlockSpec((1, sb, wv), lambda b, h, s: (b, s, h)),
        out_shape=jax.ShapeDtypeStruct((B, S, GDN_V_WIDTH), BF16),
        scratch_shapes=[pltpu.VMEM((v_heads, GDN_K_DIM, GDN_V_DIM), F32)],
        compiler_params=pltpu.CompilerParams(
            dimension_semantics=("parallel", "parallel", "arbitrary"),
            vmem_limit_bytes=_vmem_limit(blocks + (12 << 20))),
        name="gated_delta_rule",
    )(p, p, p, p, p, p, p, conv_w, conv_w, conv_w, g_cum, beta, norm_g)


def _expert_kernel(be_ref, nx_ref, nu_ref, x_ref, wgu_hbm, wd_hbm, *rest, layer):
    o_ref, wgu32_ref, wd32_ref, wgu16_ref, wd16_ref, sem_ref, run_ref = rest[-7:]
    i = pl.program_id(0)
    used = i < nu_ref[0]
    new_expert = jnp.logical_or(i == 0, be_ref[i] != be_ref[jnp.maximum(i - 1, 0)])

    def weight_copies(expert, slot):
        return (pltpu.make_async_copy(wgu_hbm.at[layer, expert], wgu32_ref.at[slot], sem_ref.at[0, slot]),
                pltpu.make_async_copy(wd_hbm.at[layer, expert], wd32_ref.at[slot], sem_ref.at[1, slot]))

    @pl.when(i == 0)
    def _():
        run_ref[0] = 0

    @pl.when(jnp.logical_and(used, i == 0))
    def _():
        for cp in weight_copies(be_ref[0], 0):
            cp.start()

    @pl.when(jnp.logical_and(used, new_expert))
    def _():
        slot = run_ref[0] % 2
        for cp in weight_copies(be_ref[i], slot):
            cp.wait()
        wgu16_ref[...] = wgu32_ref[slot].astype(BF16)
        wd16_ref[...] = wd32_ref[slot].astype(BF16)

        @pl.when(nx_ref[i] >= 0)
        def _():
            for cp in weight_copies(nx_ref[i], 1 - slot):
                cp.start()

        run_ref[0] = run_ref[0] + 1

    @pl.when(used)
    def _():
        x_lo, x_hi = _unpack_halves(x_ref[...])
        half = x_lo.shape[-1]
        gu = (jnp.dot(x_lo.astype(BF16), wgu16_ref[:half, :], preferred_element_type=F32)
              + jnp.dot(x_hi.astype(BF16), wgu16_ref[half:, :], preferred_element_type=F32))
        f = gu.shape[-1] // 2
        gate = gu[:, :f]
        act = gate * jax.nn.sigmoid(gate) * gu[:, f:]
        y = jnp.dot(act.astype(BF16), wd16_ref[...], preferred_element_type=F32)
        if o_ref.dtype == jnp.uint32:
            o_ref[...] = _pack_halves(y)
        else:
            o_ref[...] = y.astype(o_ref.dtype)

    @pl.when(jnp.logical_not(used))
    def _():
        o_ref[...] = jnp.zeros_like(o_ref)


def _grouped_swiglu(x_rows, w_gate_up, w_down, layer, blk_expert, n_used, tm, out_dtype, name,
                    out_prev=None, out_block0=0, total_blocks=None):
    R = x_rows.shape[0]
    D = w_gate_up.shape[-2]
    F2 = w_gate_up.shape[-1]
    tm = min(tm, R)
    assert R % tm == 0 and x_rows.shape[1] * 2 == D
    nb = R // tm
    grid_blocks = nb if total_blocks is None else total_blocks
    d_out = D // 2 if out_dtype == jnp.uint32 else D
    w_elems = D * F2 + (F2 // 2) * D
    need = (2 * w_elems * 4 + w_elems * 2 + 2 * _nbytes((tm, D), BF16) + 2 * _nbytes((tm, D), F32)
            + 2 * _nbytes((tm, F2), F32) + _nbytes((tm, D), F32))
    idx = jnp.arange(nb, dtype=jnp.int32)
    is_first = jnp.concatenate([jnp.ones((1,), bool), blk_expert[1:] != blk_expert[:-1]])
    next_first = lax.cummin(jnp.where(is_first, idx, nb)[::-1])[::-1]
    next_first = jnp.concatenate([next_first[1:], jnp.full((1,), nb, jnp.int32)])
    has_next = next_first < jnp.minimum(n_used[0], nb)
    nxt = jnp.where(has_next, blk_expert[jnp.minimum(next_first, nb - 1)], -1).astype(jnp.int32)
    if grid_blocks > nb:
        pad = grid_blocks - nb
        blk_expert = jnp.concatenate([blk_expert, jnp.broadcast_to(blk_expert[-1:], (pad,))])
        nxt = jnp.concatenate([nxt, jnp.full((pad,), -1, jnp.int32)])
    n_used = jnp.minimum(n_used, nb)
    in_specs = [pl.BlockSpec((tm, D // 2), lambda i, be, nx, nu: (jnp.minimum(i, nb - 1), 0)),
                pl.BlockSpec(memory_space=pl.ANY),
                pl.BlockSpec(memory_space=pl.ANY)]
    args = [blk_expert, nxt, n_used, x_rows, w_gate_up, w_down]
    aliases = {}
    out_rows = grid_blocks * tm
    if out_prev is not None:
        in_specs.append(pl.BlockSpec(memory_space=pl.ANY))
        aliases = {len(args): 0}
        args.append(out_prev)
        out_rows = out_prev.shape[0]
    grid_spec = pltpu.PrefetchScalarGridSpec(
        num_scalar_prefetch=3,
        grid=(grid_blocks,),
        in_specs=in_specs,
        out_specs=pl.BlockSpec((tm, d_out), lambda i, be, nx, nu: (i + out_block0, 0)),
        scratch_shapes=[pltpu.VMEM((2, D, F2), F32), pltpu.VMEM((2, F2 // 2, D), F32),
                        pltpu.VMEM((D, F2), BF16), pltpu.VMEM((F2 // 2, D), BF16),
                        pltpu.SemaphoreType.DMA((2, 2)), pltpu.SMEM((1,), jnp.int32)],
    )
    return pl.pallas_call(
        functools.partial(_expert_kernel, layer=layer),
        grid_spec=grid_spec,
        out_shape=jax.ShapeDtypeStruct((out_rows, d_out), out_dtype),
        input_output_aliases=aliases,
        compiler_params=pltpu.CompilerParams(
            dimension_semantics=("arbitrary",),
            vmem_limit_bytes=_vmem_limit(need + (4 << 20))),
        name=name,
    )(*args)


def _combine_kernel(x_ref, sh_ref, y_ref, gate_ref, g_ref, b_ref, o_ref, o16_ref):
    half = x_ref.shape[-1] // 2
    acc = ALPHA * x_ref[...] + sh_ref[...]
    acc_lo, acc_hi = acc[:, :half], acc[:, half:]
    gate = gate_ref[...]
    for k in range(TOP_K):
        y_lo, y_hi = _unpack_halves(y_ref[k])
        acc_lo = acc_lo + gate[:, k:k + 1] * y_lo
        acc_hi = acc_hi + gate[:, k:k + 1] * y_hi
    out = _layer_norm_rows(jnp.concatenate([acc_lo, acc_hi], axis=1), g_ref[...], b_ref[...])
    o_ref[...] = out
    o16_ref[...] = out.astype(BF16)


def _moe_combine(x, shared, y_tok, gate, ln_g, ln_b, tm=256):
    T, D = x.shape
    tm = min(tm, T)
    assert T % tm == 0
    need = 2 * (3 * _nbytes((tm, D), F32) + _nbytes((tm, D), BF16) + _nbytes((tm, TOP_K * D), BF16)) \
        + 3 * _nbytes((tm, D), F32)
    return pl.pallas_call(
        _combine_kernel,
        grid=(T // tm,),
        in_specs=[pl.BlockSpec((tm, D), lambda i: (i, 0)),
                  pl.BlockSpec((tm, D), lambda i: (i, 0)),
                  pl.BlockSpec((TOP_K, tm, D // 2), lambda i: (0, i, 0)),
                  pl.BlockSpec((tm, TOP_K), lambda i: (i, 0)),
                  pl.BlockSpec((1, D), lambda i: (0, 0)),
                  pl.BlockSpec((1, D), lambda i: (0, 0))],
        out_specs=[pl.BlockSpec((tm, D), lambda i: (i, 0)),
                   pl.BlockSpec((tm, D), lambda i: (i, 0))],
        out_shape=[jax.ShapeDtypeStruct((T, D), F32), jax.ShapeDtypeStruct((T, D), BF16)],
        compiler_params=pltpu.CompilerParams(
            dimension_semantics=("parallel",),
            vmem_limit_bytes=_vmem_limit(need + (4 << 20))),
        name="moe_combine",
    )(x, shared, y_tok, gate, ln_g.reshape(1, D), ln_b.reshape(1, D))


MLA_QK_PAD = 2 * MLA_NOPE_DIM
MLA_IN_PAD = MLA_Q_RANK + MLA_KV_RANK + 2 * MLA_NOPE_DIM


def _mla_proj_kernel(x_ref, win_ref, wq_ref, wkv_ref, qg_ref, kvg_ref, csq_ref, ck_ref, sk_ref,
                     q_ref, k_ref, v_ref):
    NP = MLA_NOPE_DIM
    c = jnp.dot(x_ref[...], win_ref[...], preferred_element_type=F32)

    def rms(t, g):
        return (t * lax.rsqrt(jnp.mean(t * t, axis=-1, keepdims=True) + RMS_EPS) * g).astype(BF16)

    cq = rms(c[:, :MLA_Q_RANK], qg_ref[...])
    ckv = rms(c[:, MLA_Q_RANK:MLA_Q_RANK + MLA_KV_RANK], kvg_ref[...])
    o = MLA_Q_RANK + MLA_KV_RANK
    k_rope = (c[:, o:o + NP] * ck_ref[...] + c[:, o + NP:o + 2 * NP] * sk_ref[...]).astype(BF16)
    low = lax.broadcasted_iota(jnp.int32, (x_ref.shape[0], NP), 1) < MLA_ROPE_DIM
    csq = csq_ref[...]
    for h in range(MLA_HEADS):
        qh = jnp.dot(cq, wq_ref[:, h * 2 * NP:(h + 1) * 2 * NP], preferred_element_type=F32)
        u = qh[:, NP:] * csq
        q_rope = jnp.where(low, u + pltpu.roll(u, MLA_ROPE_DIM, axis=1), 0.0)
        q_ref[0, h, :, :NP] = (qh[:, :NP] * MLA_Q_SCALE).astype(BF16)
        q_ref[0, h, :, NP:] = (q_rope * MLA_Q_SCALE).astype(BF16)
        kvh = jnp.dot(ckv, wkv_ref[:, h * 2 * NP:(h + 1) * 2 * NP], preferred_element_type=F32)
        k_ref[0, h, :, :NP] = kvh[:, :NP].astype(BF16)
        k_ref[0, h, :, NP:] = k_rope
        v_ref[0, h] = kvh[:, NP:].astype(BF16)


def _mla_proj(x16, w_in_ext, w_q_ext, w_kvb, q_norm_g, kv_norm_g, csq, ck, sk, batch, tm=256):
    T, D = x16.shape
    S = T // batch
    H = MLA_HEADS
    tm = min(tm, S)
    assert S % tm == 0
    nsb = S // tm
    w_bytes = (w_in_ext.size + w_q_ext.size + w_kvb.size) * 2
    out_bytes = _nbytes((H, tm, 2 * MLA_QK_PAD + MLA_V_DIM), BF16)
    need = w_bytes + 2 * _nbytes((tm, D), BF16) + 2 * out_bytes + 4 * _nbytes((tm, MLA_IN_PAD), F32)

    def const(shape):
        return pl.BlockSpec(shape, lambda i: (0, 0), pipeline_mode=pl.Buffered(1))

    def rows(width):
        return pl.BlockSpec((tm, width), lambda i: (i, 0))

    def heads(width):
        return pl.BlockSpec((1, H, tm, width), lambda i: (i // nsb, 0, i % nsb, 0))

    return pl.pallas_call(
        _mla_proj_kernel,
        grid=(T // tm,),
        in_specs=[rows(D), const(w_in_ext.shape), const(w_q_ext.shape), const(w_kvb.shape),
                  const((1, MLA_Q_RANK)), const((1, MLA_KV_RANK)),
                  rows(MLA_NOPE_DIM), rows(MLA_NOPE_DIM), rows(MLA_NOPE_DIM)],
        out_specs=[heads(MLA_QK_PAD), heads(MLA_QK_PAD), heads(MLA_V_DIM)],
        out_shape=[jax.ShapeDtypeStruct((batch, H, S, MLA_QK_PAD), BF16),
                   jax.ShapeDtypeStruct((batch, H, S, MLA_QK_PAD), BF16),
                   jax.ShapeDtypeStruct((batch, H, S, MLA_V_DIM), BF16)],
        compiler_params=pltpu.CompilerParams(
            dimension_semantics=("parallel",),
            vmem_limit_bytes=_vmem_limit(need + (6 << 20))),
        name="mla_proj",
    )(x16, w_in_ext, w_q_ext, w_kvb, q_norm_g.reshape(1, -1), kv_norm_g.reshape(1, -1), csq, ck, sk)


def _router_kernel(x_ref, wt_ref, bias_ref, e_ref, gate_ref, rank_ref, cnt_ref, carry_ref):
    E, G, K = N_EXPERTS, N_GROUPS, TOP_K
    P = E // G
    tm = x_ref.shape[0]

    @pl.when(pl.program_id(0) == 0)
    def _():
        carry_ref[...] = jnp.zeros_like(carry_ref)

    logits = lax.dot_general(wt_ref[...], x_ref[...], (((1,), (1,)), ((), ())),
                             precision=lax.Precision.HIGHEST, preferred_element_type=F32)
    scores = jax.nn.sigmoid(logits)
    biased = scores + bias_ref[...]

    def preceding(v):
        n = v.shape[0]
        idx = lax.broadcasted_iota(jnp.int32, (n, 1), 0)
        cnt = jnp.zeros(v.shape, F32)
        for j in range(n):
            row = v[j:j + 1, :]
            tie = jnp.where(idx > j, 1.0, 0.0)
            cnt = cnt + jnp.where(row > v, 1.0, jnp.where(row == v, tie, 0.0))
        return cnt

    blocks = [biased[g * P:(g + 1) * P, :] for g in range(G)]
    group_score = jnp.concatenate(
        [jnp.sum(jnp.where(preceding(b) < 2.0, b, 0.0), axis=0, keepdims=True) for b in blocks], axis=0)
    group_sel = preceding(group_score) < float(TOPK_GROUPS)
    masked = jnp.concatenate(
        [jnp.where(group_sel[g:g + 1, :], blocks[g], -jnp.inf) for g in range(G)], axis=0)
    sel = preceding(masked) < float(K)
    sel_f = jnp.where(sel, 1.0, 0.0)
    gate = jnp.where(sel, scores, 0.0)
    gate = gate / (jnp.sum(gate, axis=0, keepdims=True) + 1e-20) * ROUTED_SCALE

    t_row = lax.broadcasted_iota(jnp.int32, (tm, tm), 0)
    t_col = lax.broadcasted_iota(jnp.int32, (tm, tm), 1)
    before_t = jnp.where(t_row < t_col, 1.0, 0.0).astype(BF16)
    sel16 = sel_f.astype(BF16)
    rank = jnp.dot(sel16, before_t, preferred_element_type=F32) + carry_ref[...]
    carry_ref[...] = carry_ref[...] + jnp.sum(sel_f, axis=1, keepdims=True)
    e_row = lax.broadcasted_iota(jnp.int32, (E, E), 0)
    e_col = lax.broadcasted_iota(jnp.int32, (E, E), 1)
    before_e = jnp.where(e_col < e_row, 1.0, 0.0).astype(BF16)
    slot = jnp.where(sel, jnp.dot(before_e, sel16, preferred_element_type=F32), -1.0)
    e_idx = lax.broadcasted_iota(jnp.int32, (E, 1), 0).astype(F32)

    def pick(k, t):
        return jnp.sum(jnp.where(slot == float(k), t, 0.0), axis=0, keepdims=True)

    e_ref[...] = jnp.concatenate([pick(k, e_idx) for k in range(K)], axis=0).astype(jnp.int32)
    rank_ref[...] = jnp.concatenate([pick(k, rank) for k in range(K)], axis=0).astype(jnp.int32)
    gate_ref[...] = jnp.concatenate([pick(k, gate) for k in range(K)], axis=0)
    cnt_ref[...] = jnp.broadcast_to(carry_ref[...], cnt_ref.shape).astype(jnp.int32)


def _router(x, router_wt, router_bias, tm=512):
    T, D = x.shape
    E = router_wt.shape[0]
    tm = min(tm, T)
    assert T % tm == 0
    kt = pl.BlockSpec((TOP_K, tm), lambda i: (0, i))
    top_e, gate, rank, cnt = pl.pallas_call(
        _router_kernel,
        grid=(T // tm,),
        in_specs=[pl.BlockSpec((tm, D), lambda i: (i, 0)),
                  pl.BlockSpec((E, D), lambda i: (0, 0)),
                  pl.BlockSpec((E, 1), lambda i: (0, 0))],
        out_specs=[kt, kt, kt, pl.BlockSpec((E, 128), lambda i: (0, 0))],
        out_shape=[jax.ShapeDtypeStruct((TOP_K, T), jnp.int32), jax.ShapeDtypeStruct((TOP_K, T), F32),
                   jax.ShapeDtypeStruct((TOP_K, T), jnp.int32), jax.ShapeDtypeStruct((E, 128), jnp.int32)],
        scratch_shapes=[pltpu.VMEM((E, 1), F32)],
        compiler_params=pltpu.CompilerParams(
            dimension_semantics=("arbitrary",),
            vmem_limit_bytes=_vmem_limit(2 * _nbytes((tm, D), F32) + (16 << 20))),
        name="moe_router",
    )(x, router_wt, router_bias.reshape(E, 1))
    return top_e, gate, rank, cnt[:, 0]


def _mla_mixer(x, x16, positions, w_in, q_norm_g, kv_norm_g, w_qb, w_kvb, w_o, ln_g, ln_b):
    B, S = positions.shape
    T, D = x.shape
    H = MLA_HEADS
    half = MLA_ROPE_DIM // 2

    def rot(w):
        return jnp.concatenate([-w[..., half:], w[..., :half]], axis=-1)

    lat = MLA_Q_RANK + MLA_KV_RANK
    w_kr = w_in[:, lat:]
    zpad = jnp.zeros((D, MLA_NOPE_DIM - MLA_ROPE_DIM), w_in.dtype)
    w_in_ext = jnp.concatenate([w_in[:, :lat], w_kr, zpad, rot(w_kr), zpad], axis=1).astype(BF16)
    w_q3 = w_qb.reshape(MLA_Q_RANK, H, MLA_QK_DIM)
    w_q_rope = w_q3[..., MLA_NOPE_DIM:]
    w_q_ext = jnp.concatenate([w_q3[..., :MLA_NOPE_DIM], w_q_rope, rot(w_q_rope)], axis=-1)
    w_q_ext = w_q_ext.reshape(MLA_Q_RANK, H * MLA_QK_PAD).astype(BF16)
    inv_freq = 1.0 / (ROPE_THETA ** (jnp.arange(0, MLA_ROPE_DIM, 2, dtype=F32) / MLA_ROPE_DIM))
    ang = positions.astype(F32).reshape(T, 1) * inv_freq
    cos, sin = jnp.cos(ang), jnp.sin(ang)
    zero = jnp.zeros_like(cos)
    csq = jnp.concatenate([cos, cos, sin, sin], axis=1)
    ck = jnp.concatenate([cos, cos, zero, zero], axis=1)
    sk = jnp.concatenate([sin, sin, zero, zero], axis=1)
    q_cat, k_cat, v = _mla_proj(x16, w_in_ext, w_q_ext, w_kvb.astype(BF16), q_norm_g, kv_norm_g,
                                csq, ck, sk, B)
    o = _attention(q_cat, k_cat, v)
    return _matmul_ln(o.reshape(T, H * MLA_V_DIM), w_o.astype(BF16), x, ln_g, ln_b, 512, "mla_out_ln")


def _gdn_mixer(x, x16, batch, w_in, conv_w, a_log, dt_bias, norm_g, w_out, ln_g, ln_b):
    T, D = x.shape
    B, S = batch, T // batch
    n_main = GDN_CONV_DIM + GDN_V_WIDTH
    p = _matmul(x16, w_in[:, :n_main].astype(BF16), F32, name="gdn_in")
    ba = _matmul(x16, w_in[:, n_main:].astype(BF16), F32, name="gdn_in_ba")
    b = ba[:, :GDN_V_HEADS].reshape(B, S, GDN_V_HEADS)
    a = ba[:, GDN_V_HEADS:].reshape(B, S, GDN_V_HEADS)
    beta = jax.nn.sigmoid(b)
    g = -jnp.exp(a_log) * jax.nn.softplus(a + dt_bias)
    N = S // CHUNK
    g_cum = jnp.cumsum(g.reshape(B, N, CHUNK, GDN_V_HEADS), axis=2)
    g_cum = jnp.transpose(g_cum, (0, 3, 1, 2))
    beta = jnp.transpose(beta.reshape(B, N, CHUNK, GDN_V_HEADS), (0, 3, 1, 2))
    o = _gated_delta_rule(p.reshape(B, S, n_main), conv_w, g_cum, beta, norm_g.reshape(1, GDN_V_DIM))
    return _matmul_ln(o.reshape(T, GDN_V_WIDTH), w_out.astype(BF16), x, ln_g, ln_b, 256, "gdn_out_ln")


def _moe_ffn(x, xpk, layer, router_w, router_bias, w_gate_up, w_down, shared_gate_up, shared_down, ln_g, ln_b):
    T, D = x.shape
    E, K = N_EXPERTS, TOP_K
    top_e, gate, rank, counts = _router(x, router_w.T, router_bias)

    zero = jnp.zeros((1,), jnp.int32)
    tm_sh = min(512, T)
    shared = _grouped_swiglu(xpk, shared_gate_up, shared_down, layer,
                             jnp.zeros((T // tm_sh,), jnp.int32), zero + T // tm_sh, tm_sh, F32,
                             "shared_expert")

    n_assign = T * K
    n_blocks = -(-n_assign // MOE_BLOCK) + E
    starts = jnp.cumsum(counts) - counts
    padded = (counts + MOE_BLOCK - 1) // MOE_BLOCK * MOE_BLOCK
    pad_ends = jnp.cumsum(padded)
    pad_starts = pad_ends - padded
    e_ids = jnp.arange(E, dtype=jnp.int32)
    pos = jnp.sum(jnp.where(top_e[..., None] == e_ids, pad_starts, 0), axis=-1) + rank
    blk_first = jnp.arange(n_blocks, dtype=jnp.int32) * MOE_BLOCK
    blk_expert = jnp.minimum(jnp.sum((pad_ends[None, :] <= blk_first[:, None]).astype(jnp.int32), axis=1),
                             E - 1)
    n_used = (pad_ends[-1:] // MOE_BLOCK).astype(jnp.int32)
    tok = jnp.arange(T, dtype=jnp.int32)
    order = jnp.sort((top_e * T + tok[None, :]).reshape(n_assign)) % T
    row_j = (blk_first - pad_starts[blk_expert])[:, None] + jnp.arange(MOE_BLOCK, dtype=jnp.int32)[None, :]
    row_src = jnp.clip(starts[blk_expert][:, None] + row_j, 0, n_assign - 1)
    pad_tok = (blk_first[:, None] + jnp.arange(MOE_BLOCK, dtype=jnp.int32)[None, :]) % T
    row_tok = jnp.where(row_j < counts[blk_expert][:, None], order[row_src], pad_tok)

    n_chunks = MOE_GATHER_CHUNKS if n_blocks % MOE_GATHER_CHUNKS == 0 else 1
    cb = n_blocks // n_chunks
    y_rows = None
    for c in range(n_chunks):
        x_rows = xpk.at[row_tok[c * cb:(c + 1) * cb].reshape(-1)].get(mode="promise_in_bounds")
        y_rows = _grouped_swiglu(x_rows, w_gate_up, w_down, layer, blk_expert[c * cb:(c + 1) * cb],
                                 jnp.clip(n_used - c * cb, 0, cb), MOE_BLOCK, jnp.uint32, "routed_experts",
                                 out_prev=y_rows, out_block0=c * cb,
                                 total_blocks=n_blocks if c == 0 else None)
    y_tok = y_rows.at[pos.reshape(n_assign)].get(mode="promise_in_bounds").reshape(K, T, D // 2)
    return _moe_combine(x, shared, y_tok, gate.T, ln_g, ln_b)


def kernel(x, positions, mla_w_in, mla_q_norm, mla_kv_norm, mla_w_qb, mla_w_kvb, mla_w_o, gdn_w_in, gdn_conv_w, gdn_a_log, gdn_dt_bias, gdn_norm, gdn_w_out, ln_mix_g, ln_mix_b, ln_ffn_g, ln_ffn_b, moe_router, moe_router_bias, moe_w_gate_up, moe_w_down, moe_shared_gate_up, moe_shared_down):
    B, S, D = x.shape
    x = x.reshape(B * S, D)
    x16 = x.astype(BF16)
    for i in range(DEPTH):
        j = i // 2
        if i % 2 == 0:
            x, xpk = _mla_mixer(x, x16, positions, mla_w_in[j], mla_q_norm[j], mla_kv_norm[j],
                                mla_w_qb[j], mla_w_kvb[j], mla_w_o[j], ln_mix_g[i], ln_mix_b[i])
        else:
            x, xpk = _gdn_mixer(x, x16, B, gdn_w_in[j], gdn_conv_w[j], gdn_a_log[j], gdn_dt_bias[j],
                                gdn_norm[j], gdn_w_out[j], ln_mix_g[i], ln_mix_b[i])
        x, x16 = _moe_ffn(x, xpk, i, moe_router[i], moe_router_bias[i], moe_w_gate_up, moe_w_down,
                          moe_shared_gate_up[:, None], moe_shared_down[:, None], ln_ffn_g[i], ln_ffn_b[i])
    return x.reshape(B, S, D)
```

```python
import functools
import math

import jax
import jax.numpy as jnp
from jax import lax
from jax.experimental import pallas as pl
from jax.experimental.pallas import tpu as pltpu

F32 = jnp.float32
BF16 = jnp.bfloat16

DEPTH = 2
MLA_HEADS = 16
MLA_Q_RANK = 512
MLA_KV_RANK = 512
MLA_NOPE_DIM = 128
MLA_ROPE_DIM = 64
MLA_V_DIM = 128
MLA_QK_DIM = MLA_NOPE_DIM + MLA_ROPE_DIM
MLA_SCALE = MLA_QK_DIM ** -0.5
MLA_Q_SCALE = MLA_SCALE * math.log2(math.e)
ROPE_THETA = 10000.0

GDN_K_HEADS = 16
GDN_V_HEADS = 32
GDN_K_DIM = 128
GDN_V_DIM = 128
CONV_K = 4
CHUNK = 64
GDN_QK_WIDTH = GDN_K_HEADS * GDN_K_DIM
GDN_V_WIDTH = GDN_V_HEADS * GDN_V_DIM
GDN_CONV_DIM = 2 * GDN_QK_WIDTH + GDN_V_WIDTH

N_EXPERTS = 64
TOP_K = 8
N_GROUPS = 8
TOPK_GROUPS = 4
EXPERT_DIM = 512
ROUTED_SCALE = 2.5
MOE_BLOCK = 256
MOE_GATHER_CHUNKS = 4

ALPHA = (2 * DEPTH) ** 0.25
LN_EPS = 1e-5
RMS_EPS = 1e-6

V7X_VMEM_BYTES = 64 * 1024 * 1024
VMEM_LIMIT_CAP = 56 * 1024 * 1024
NEG_BIG = -1e30


def _vmem_limit(nbytes):
    return int(min(max(nbytes, 16 * 1024 * 1024), VMEM_LIMIT_CAP))


def _nbytes(shape, dtype):
    return math.prod(shape) * jnp.dtype(dtype).itemsize


def _layer_norm_rows(y, g, b):
    mu = jnp.mean(y, axis=-1, keepdims=True)
    yc = y - mu
    var = jnp.mean(yc * yc, axis=-1, keepdims=True)
    return yc * lax.rsqrt(var + LN_EPS) * g + b


def _pack_halves(x):
    n = x.shape[-1] // 2
    lo = pltpu.bitcast(x[:, :n].astype(BF16).astype(F32), jnp.uint32)
    hi = pltpu.bitcast(x[:, n:].astype(BF16).astype(F32), jnp.uint32)
    return hi | (lo >> 16)


def _unpack_halves(w):
    lo = pltpu.bitcast(w << 16, F32)
    hi = pltpu.bitcast(w & jnp.uint32(0xFFFF0000), F32)
    return lo, hi


def _mm_kernel(x_ref, w_ref, o_ref):
    o_ref[...] = jnp.dot(x_ref[...], w_ref[...], preferred_element_type=F32).astype(o_ref.dtype)


def _matmul(x, w, out_dtype, tm=1024, tn=1024, name="matmul"):
    M, K = x.shape
    N = w.shape[1]
    tm = min(tm, M)
    tn = min(tn, N)
    assert M % tm == 0 and N % tn == 0, (M, N, tm, tn)
    need = 2 * (_nbytes((tm, K), x.dtype) + _nbytes((K, tn), w.dtype) + _nbytes((tm, tn), out_dtype))
    need += _nbytes((tm, tn), F32)
    return pl.pallas_call(
        _mm_kernel,
        grid=(N // tn, M // tm),
        in_specs=[pl.BlockSpec((tm, K), lambda j, i: (i, 0)),
                  pl.BlockSpec((K, tn), lambda j, i: (0, j))],
        out_specs=pl.BlockSpec((tm, tn), lambda j, i: (i, j)),
        out_shape=jax.ShapeDtypeStruct((M, N), out_dtype),
        compiler_params=pltpu.CompilerParams(
            dimension_semantics=("parallel", "parallel"),
            vmem_limit_bytes=_vmem_limit(need + (4 << 20))),
        name=name,
    )(x, w)


def _mm_ln_kernel(x_ref, w_ref, r_ref, g_ref, b_ref, o_ref, opk_ref):
    y = ALPHA * r_ref[...] + jnp.dot(x_ref[...], w_ref[...], preferred_element_type=F32)
    out = _layer_norm_rows(y, g_ref[...], b_ref[...])
    o_ref[...] = out
    opk_ref[...] = _pack_halves(out)


def _matmul_ln(x, w, resid, ln_g, ln_b, tm, name):
    M, K = x.shape
    N = w.shape[1]
    tm = min(tm, M)
    assert M % tm == 0
    need = (_nbytes((K, N), BF16) + 2 * _nbytes((tm, K), BF16) + 4 * _nbytes((tm, N), F32)
            + 2 * _nbytes((tm, N), BF16) + 2 * _nbytes((tm, N), F32))
    return pl.pallas_call(
        _mm_ln_kernel,
        grid=(M // tm,),
        in_specs=[pl.BlockSpec((tm, K), lambda i: (i, 0)),
                  pl.BlockSpec((K, N), lambda i: (0, 0), pipeline_mode=pl.Buffered(1)),
                  pl.BlockSpec((tm, N), lambda i: (i, 0)),
                  pl.BlockSpec((1, N), lambda i: (0, 0)),
                  pl.BlockSpec((1, N), lambda i: (0, 0))],
        out_specs=[pl.BlockSpec((tm, N), lambda i: (i, 0)),
                   pl.BlockSpec((tm, N // 2), lambda i: (i, 0))],
        out_shape=[jax.ShapeDtypeStruct((M, N), F32), jax.ShapeDtypeStruct((M, N // 2), jnp.uint32)],
        compiler_params=pltpu.CompilerParams(
            dimension_semantics=("parallel",),
            vmem_limit_bytes=_vmem_limit(need + (4 << 20))),
        name=name,
    )(x, w, resid, ln_g.reshape(1, N), ln_b.reshape(1, N))


def _attn_kernel(q_ref, k_ref, v_ref, o_ref, *, blk, wide):
    i = pl.program_id(2)
    q = q_ref[0, 0]

    def step(carry, off, width, mask):
        m, l, acc = carry
        k = k_ref[0, 0, pl.ds(off, width), :]
        s = lax.dot_general(q, k, (((1,), (1,)), ((), ())), preferred_element_type=F32)
        if mask is not None:
            s = jnp.where(mask, s, NEG_BIG)
        m_new = jnp.maximum(m, jnp.max(s, axis=-1, keepdims=True))
        alpha = jnp.exp2(m - m_new)
        p = jnp.exp2(s - m_new)
        l = alpha * l + jnp.sum(p, axis=-1, keepdims=True)
        v = v_ref[0, 0, pl.ds(off, width), :]
        acc = alpha * acc + jnp.dot(p.astype(BF16), v, preferred_element_type=F32)
        return m_new, l, acc

    carry = (jnp.full((blk, 1), NEG_BIG, F32), jnp.zeros((blk, 1), F32), jnp.zeros((blk, MLA_V_DIM), F32))
    n_wide = i // wide
    carry = lax.fori_loop(
        0, n_wide, lambda j, c: step(c, pl.multiple_of(j * (wide * blk), wide * blk), wide * blk, None), carry)
    carry = lax.fori_loop(
        n_wide * wide, i, lambda j, c: step(c, pl.multiple_of(j * blk, blk), blk, None), carry)
    row = lax.broadcasted_iota(jnp.int32, (blk, blk), 0)
    col = lax.broadcasted_iota(jnp.int32, (blk, blk), 1)
    m, l, acc = step(carry, pl.multiple_of(i * blk, blk), blk, row >= col)
    o_ref[0] = (acc / l).astype(o_ref.dtype)


def _attention(q, k, v, blk=1024, wide=1):
    B, H, S, qk_w = q.shape
    blk = min(blk, S)
    assert S % blk == 0
    need = 2 * (_nbytes((blk, 256), BF16) + _nbytes((S, 256), BF16) + _nbytes((S, 128), BF16)
                + _nbytes((blk, 128), BF16)) + 6 * _nbytes((blk, blk), F32)
    return pl.pallas_call(
        functools.partial(_attn_kernel, blk=blk, wide=wide),
        grid=(B, H, S // blk),
        in_specs=[pl.BlockSpec((1, 1, blk, qk_w), lambda b, h, i: (b, h, i, 0)),
                  pl.BlockSpec((1, 1, S, qk_w), lambda b, h, i: (b, h, 0, 0)),
                  pl.BlockSpec((1, 1, S, MLA_V_DIM), lambda b, h, i: (b, h, 0, 0))],
        out_specs=pl.BlockSpec((1, blk, MLA_V_DIM), lambda b, h, i: (b, i, h)),
        out_shape=jax.ShapeDtypeStruct((B, S, H * MLA_V_DIM), BF16),
        compiler_params=pltpu.CompilerParams(
            dimension_semantics=("parallel", "parallel", "parallel"),
            vmem_limit_bytes=_vmem_limit(need + (8 << 20))),
        name="mla_attention",
    )(q, k, v)


def _gdn_kernel(pq_ref, pk_ref, pv_ref, pz_ref, hq_ref, hk_ref, hv_ref, cq_ref, ck_ref, cv_ref,
                g_ref, b_ref, ng_ref, o_ref, state_ref, *, n_chunks, k_heads):
    C = CHUNK
    rep = GDN_V_HEADS // GDN_K_HEADS
    v_heads = k_heads * rep
    DK, DV = GDN_K_DIM, GDN_V_DIM
    first_block = pl.program_id(2) == 0

    @pl.when(first_block)
    def _():
        state_ref[...] = jnp.zeros_like(state_ref)

    row = lax.broadcasted_iota(jnp.int32, (C, C), 0)
    col = lax.broadcasted_iota(jnp.int32, (C, C), 1)
    causal = row >= col
    strict = row > col
    eye = row == col
    eye_f = eye.astype(F32)
    nt = (((1,), (1,)), ((), ()))
    tn = (((0,), (0,)), ((), ()))

    def dot(a, b):
        return jnp.dot(a, b, preferred_element_type=F32)

    def conv_silu(x_ref, halo_ref, w_ref, c, r0):
        cur = x_ref[0, pl.ds(r0, C), :]
        prev_start = pl.multiple_of(jnp.maximum(c * (C // 8) - 1, 0) * 8, 8)
        prev_rows = x_ref[0, pl.ds(prev_start, 8), :]
        halo = jnp.where(first_block, 0.0, halo_ref[0])
        ext = jnp.concatenate([jnp.where(c == 0, halo, prev_rows), cur], axis=0)
        acc = cur * w_ref[CONV_K - 1:CONV_K, :]
        for j in range(1, CONV_K):
            acc = acc + pltpu.roll(ext, j, axis=0)[8:] * w_ref[CONV_K - 1 - j:CONV_K - j, :]
        return acc * jax.nn.sigmoid(acc)

    def l2n(t):
        return t * lax.rsqrt(jnp.sum(t * t, axis=-1, keepdims=True) + RMS_EPS)

    def chunk(c, carry):
        r0 = pl.multiple_of(c * C, C)
        qa = conv_silu(pq_ref, hq_ref, cq_ref, c, r0)
        ka = conv_silu(pk_ref, hk_ref, ck_ref, c, r0)
        va = conv_silu(pv_ref, hv_ref, cv_ref, c, r0)
        q = [l2n(qa[:, i * DK:(i + 1) * DK]) * (DK ** -0.5) for i in range(k_heads)]
        k = [l2n(ka[:, i * DK:(i + 1) * DK]) for i in range(k_heads)]
        v = [va[:, h * DV:(h + 1) * DV] for h in range(v_heads)]
        k16 = [t.astype(BF16) for t in k]
        qk16 = [jnp.concatenate([q[i], k[i]], axis=0).astype(BF16) for i in range(k_heads)]
        qkk = [lax.dot_general(qk16[i], k16[i], nt, preferred_element_type=F32) for i in range(k_heads)]

        g_row = [g_ref[0, h, pl.ds(c, 1), :] for h in range(v_heads)]
        b_row = [b_ref[0, h, pl.ds(c, 1), :] for h in range(v_heads)]
        g_col = [jnp.sum(jnp.where(eye, t, 0.0), axis=1, keepdims=True) for t in g_row]
        b_col = [jnp.sum(jnp.where(eye, t, 0.0), axis=1, keepdims=True) for t in b_row]
        g_last = [t[:, C - 1:C] for t in g_row]
        decay = [jnp.exp(jnp.where(causal, g_col[h] - g_row[h], NEG_BIG)) for h in range(v_heads)]
        a = [jnp.where(strict, qkk[h // rep][C:] * b_col[h] * decay[h], 0.0) for h in range(v_heads)]
        qkd = [(qkk[h // rep][:C] * decay[h]).astype(BF16) for h in range(v_heads)]

        a16 = [t.astype(BF16) for t in a]
        x = [dot(t, t) for t in a16]
        p = [eye_f - t for t in a]
        for _ in range(int(math.log2(C)) - 2):
            x16 = [t.astype(BF16) for t in x]
            px = [dot(jnp.concatenate([p[h].astype(BF16), x16[h]], axis=0), x16[h]) for h in range(v_heads)]
            p = [p[h] + px[h][:C] for h in range(v_heads)]
            x = [px[h][C:] for h in range(v_heads)]
        p = [p[h] + dot(p[h].astype(BF16), x[h].astype(BF16)) for h in range(v_heads)]

        eg = [jnp.exp(t) for t in g_col]
        rhs = [jnp.concatenate([v[h] * b_col[h], k[h // rep] * (b_col[h] * eg[h])], axis=1).astype(BF16)
               for h in range(v_heads)]
        sol = [dot(p[h].astype(BF16), rhs[h]) for h in range(v_heads)]
        s = [state_ref[h] for h in range(v_heads)]
        wq = [jnp.concatenate([sol[h][:, DV:], q[h // rep] * eg[h]], axis=0).astype(BF16)
              for h in range(v_heads)]
        ws = [dot(wq[h], s[h].astype(BF16)) for h in range(v_heads)]
        u16 = [(sol[h][:, :DV] - ws[h][:C]).astype(BF16) for h in range(v_heads)]
        k_dec = [(k[h // rep] * jnp.exp(g_last[h] - g_col[h])).astype(BF16) for h in range(v_heads)]
        ds = [lax.dot_general(k_dec[h], u16[h], tn, preferred_element_type=F32) for h in range(v_heads)]
        o = [ws[h][C:] + dot(qkd[h], u16[h]) for h in range(v_heads)]
        for h in range(v_heads):
            state_ref[h] = s[h] * jnp.exp(g_last[h]) + ds[h]
            z = pz_ref[0, pl.ds(r0, C), h * DV:(h + 1) * DV]
            on = o[h] * lax.rsqrt(jnp.mean(o[h] * o[h], axis=-1, keepdims=True) + RMS_EPS) * ng_ref[...]
            o_ref[0, pl.ds(r0, C), h * DV:(h + 1) * DV] = (on * (z * jax.nn.sigmoid(z))).astype(o_ref.dtype)
        return carry

    lax.fori_loop(0, n_chunks, chunk, 0)


def _gated_delta_rule(p, conv_w, g_cum, beta, norm_g, n_chunks=8, k_heads=8):
    B, S, _ = p.shape
    n_chunks = min(n_chunks, S // CHUNK)
    sb = n_chunks * CHUNK
    rep = GDN_V_HEADS // GDN_K_HEADS
    v_heads = k_heads * rep
    wq, wv = k_heads * GDN_K_DIM, v_heads * GDN_V_DIM
    assert S % sb == 0 and GDN_K_HEADS % k_heads == 0
    k_off, v_off, z_off = GDN_QK_WIDTH // wq, 2 * GDN_QK_WIDTH // wv, GDN_CONV_DIM // wv
    hb = sb // 8

    def main(width, off):
        return pl.BlockSpec((1, sb, width), lambda b, h, s: (b, s, off + h))

    def halo(width, off):
        return pl.BlockSpec((1, 8, width), lambda b, h, s: (b, jnp.maximum(s * hb - 1, 0), off + h))

    def cw(width, off):
        return pl.BlockSpec((CONV_K, width), lambda b, h, s: (0, off + h))

    blocks = 2 * sb * (2 * wq + 2 * wv) * 4 + 2 * sb * wv * 2
    return pl.pallas_call(
        functools.partial(_gdn_kernel, n_chunks=n_chunks, k_heads=k_heads),
        grid=(B, GDN_K_HEADS // k_heads, S // sb),
        in_specs=[main(wq, 0), main(wq, k_off), main(wv, v_off), main(wv, z_off),
                  halo(wq, 0), halo(wq, k_off), halo(wv, v_off),
                  cw(wq, 0), cw(wq, k_off), cw(wv, v_off),
                  pl.BlockSpec((1, v_heads, n_chunks, CHUNK), lambda b, h, s: (b, h, s, 0)),
                  pl.BlockSpec((1, v_heads, n_chunks, CHUNK), lambda b, h, s: (b, h, s, 0)),
                  pl.BlockSpec((1, GDN_V_DIM), lambda b, h, s: (0, 0))],
        out_specs=pl.BlockSpec((1, sb, wv), lambda b, h, s: (b, s, h)),
        out_shape=jax.ShapeDtypeStruct((B, S, GDN_V_WIDTH), BF16),
        scratch_shapes=[pltpu.VMEM((v_heads, GDN_K_DIM, GDN_V_DIM), F32)],
        compiler_params=pltpu.CompilerParams(
            dimension_semantics=("parallel", "parallel", "arbitrary"),
            vmem_limit_bytes=_vmem_limit(blocks + (12 << 20))),
        name="gated_delta_rule",
    )(p, p, p, p, p, p, p, conv_w, conv_w, conv_w, g_cum, beta, norm_g)


def _expert_kernel(be_ref, nx_ref, nu_ref, x_ref, wgu_hbm, wd_hbm, *rest, layer):
    o_ref, wgu32_ref, wd32_ref, wgu16_ref, wd16_ref, sem_ref, run_ref = rest[-7:]
    i = pl.program_id(0)
    used = i < nu_ref[0]
    new_expert = jnp.logical_or(i == 0, be_ref[i] != be_ref[jnp.maximum(i - 1, 0)])

    def weight_copies(expert, slot):
        return (pltpu.make_async_copy(wgu_hbm.at[layer, expert], wgu32_ref.at[slot], sem_ref.at[0, slot]),
                pltpu.make_async_copy(wd_hbm.at[layer, expert], wd32_ref.at[slot], sem_ref.at[1, slot]))

    @pl.when(i == 0)
    def _():
        run_ref[0] = 0

    @pl.when(jnp.logical_and(used, i == 0))
    def _():
        for cp in weight_copies(be_ref[0], 0):
            cp.start()

    @pl.when(jnp.logical_and(used, new_expert))
    def _():
        slot = run_ref[0] % 2
        for cp in weight_copies(be_ref[i], slot):
            cp.wait()
        wgu16_ref[...] = wgu32_ref[slot].astype(BF16)
        wd16_ref[...] = wd32_ref[slot].astype(BF16)

        @pl.when(nx_ref[i] >= 0)
        def _():
            for cp in weight_copies(nx_ref[i], 1 - slot):
                cp.start()

        run_ref[0] = run_ref[0] + 1

    @pl.when(used)
    def _():
        x_lo, x_hi = _unpack_halves(x_ref[...])
        half = x_lo.shape[-1]
        gu = (jnp.dot(x_lo.astype(BF16), wgu16_ref[:half, :], preferred_element_type=F32)
              + jnp.dot(x_hi.astype(BF16), wgu16_ref[half:, :], preferred_element_type=F32))
        f = gu.shape[-1] // 2
        gate = gu[:, :f]
        act = gate * jax.nn.sigmoid(gate) * gu[:, f:]
        y = jnp.dot(act.astype(BF16), wd16_ref[...], preferred_element_type=F32)
        if o_ref.dtype == jnp.uint32:
            o_ref[...] = _pack_halves(y)
        else:
            o_ref[...] = y.astype(o_ref.dtype)

    @pl.when(jnp.logical_not(used))
    def _():
        o_ref[...] = jnp.zeros_like(o_ref)


def _grouped_swiglu(x_rows, w_gate_up, w_down, layer, blk_expert, n_used, tm, out_dtype, name,
                    out_prev=None, out_block0=0, total_blocks=None):
    R = x_rows.shape[0]
    D = w_gate_up.shape[-2]
    F2 = w_gate_up.shape[-1]
    tm = min(tm, R)
    assert R % tm == 0 and x_rows.shape[1] * 2 == D
    nb = R // tm
    grid_blocks = nb if total_blocks is None else total_blocks
    d_out = D // 2 if out_dtype == jnp.uint32 else D
    w_elems = D * F2 + (F2 // 2) * D
    need = (2 * w_elems * 4 + w_elems * 2 + 2 * _nbytes((tm, D), BF16) + 2 * _nbytes((tm, D), F32)
            + 2 * _nbytes((tm, F2), F32) + _nbytes((tm, D), F32))
    idx = jnp.arange(nb, dtype=jnp.int32)
    is_first = jnp.concatenate([jnp.ones((1,), bool), blk_expert[1:] != blk_expert[:-1]])
    next_first = lax.cummin(jnp.where(is_first, idx, nb)[::-1])[::-1]
    next_first = jnp.concatenate([next_first[1:], jnp.full((1,), nb, jnp.int32)])
    has_next = next_first < jnp.minimum(n_used[0], nb)
    nxt = jnp.where(has_next, blk_expert[jnp.minimum(next_first, nb - 1)], -1).astype(jnp.int32)
    if grid_blocks > nb:
        pad = grid_blocks - nb
        blk_expert = jnp.concatenate([blk_expert, jnp.broadcast_to(blk_expert[-1:], (pad,))])
        nxt = jnp.concatenate([nxt, jnp.full((pad,), -1, jnp.int32)])
    n_used = jnp.minimum(n_used, nb)
    in_specs = [pl.BlockSpec((tm, D // 2), lambda i, be, nx, nu: (jnp.minimum(i, nb - 1), 0)),
                pl.BlockSpec(memory_space=pl.ANY),
                pl.BlockSpec(memory_space=pl.ANY)]
    args = [blk_expert, nxt, n_used, x_rows, w_gate_up, w_down]
    aliases = {}
    out_rows = grid_blocks * tm
    if out_prev is not None:
        in_specs.append(pl.BlockSpec(memory_space=pl.ANY))
        aliases = {len(args): 0}
        args.append(out_prev)
        out_rows = out_prev.shape[0]
    grid_spec = pltpu.PrefetchScalarGridSpec(
        num_scalar_prefetch=3,
        grid=(grid_blocks,),
        in_specs=in_specs,
        out_specs=pl.BlockSpec((tm, d_out), lambda i, be, nx, nu: (i + out_block0, 0)),
        scratch_shapes=[pltpu.VMEM((2, D, F2), F32), pltpu.VMEM((2, F2 // 2, D), F32),
                        pltpu.VMEM((D, F2), BF16), pltpu.VMEM((F2 // 2, D), BF16),
                        pltpu.SemaphoreType.DMA((2, 2)), pltpu.SMEM((1,), jnp.int32)],
    )
    return pl.pallas_call(
        functools.partial(_expert_kernel, layer=layer),
        grid_spec=grid_spec,
        out_shape=jax.ShapeDtypeStruct((out_rows, d_out), out_dtype),
        input_output_aliases=aliases,
        compiler_params=pltpu.CompilerParams(
            dimension_semantics=("arbitrary",),
            vmem_limit_bytes=_vmem_limit(need + (4 << 20))),
        name=name,
    )(*args)


def _zeros_kernel(o_ref):
    o_ref[...] = jnp.zeros_like(o_ref)


def _zero_rows(rows, cols, dtype, block_rows):
    block_rows = block_rows if rows % block_rows == 0 else MOE_BLOCK
    return pl.pallas_call(
        _zeros_kernel,
        grid=(rows // block_rows,),
        out_specs=pl.BlockSpec((block_rows, cols), lambda i: (i, 0)),
        out_shape=jax.ShapeDtypeStruct((rows, cols), dtype),
        compiler_params=pltpu.CompilerParams(
            dimension_semantics=("parallel",),
            vmem_limit_bytes=_vmem_limit(2 * _nbytes((block_rows, cols), dtype) + (4 << 20))),
        name="zero_rows",
    )()


def _combine_kernel(x_ref, sh_ref, y_ref, gate_ref, g_ref, b_ref, o_ref, o16_ref):
    half = x_ref.shape[-1] // 2
    acc = ALPHA * x_ref[...] + sh_ref[...]
    acc_lo, acc_hi = acc[:, :half], acc[:, half:]
    gate = gate_ref[...]
    for k in range(TOP_K):
        y_lo, y_hi = _unpack_halves(y_ref[k])
        acc_lo = acc_lo + gate[:, k:k + 1] * y_lo
        acc_hi = acc_hi + gate[:, k:k + 1] * y_hi
    out = _layer_norm_rows(jnp.concatenate([acc_lo, acc_hi], axis=1), g_ref[...], b_ref[...])
    o_ref[...] = out
    o16_ref[...] = out.astype(BF16)


def _moe_combine(x, shared, y_tok, gate, ln_g, ln_b, tm=256):
    T, D = x.shape
    tm = min(tm, T)
    assert T % tm == 0
    need = 2 * (3 * _nbytes((tm, D), F32) + _nbytes((tm, D), BF16) + _nbytes((tm, TOP_K * D), BF16)) \
        + 3 * _nbytes((tm, D), F32)
    return pl.pallas_call(
        _combine_kernel,
        grid=(T // tm,),
        in_specs=[pl.BlockSpec((tm, D), lambda i: (i, 0)),
                  pl.BlockSpec((tm, D), lambda i: (i, 0)),
                  pl.BlockSpec((TOP_K, tm, D // 2), lambda i: (0, i, 0)),
                  pl.BlockSpec((tm, TOP_K), lambda i: (i, 0)),
                  pl.BlockSpec((1, D), lambda i: (0, 0)),
                  pl.BlockSpec((1, D), lambda i: (0, 0))],
        out_specs=[pl.BlockSpec((tm, D), lambda i: (i, 0)),
                   pl.BlockSpec((tm, D), lambda i: (i, 0))],
        out_shape=[jax.ShapeDtypeStruct((T, D), F32), jax.ShapeDtypeStruct((T, D), BF16)],
        compiler_params=pltpu.CompilerParams(
            dimension_semantics=("parallel",),
            vmem_limit_bytes=_vmem_limit(need + (4 << 20))),
        name="moe_combine",
    )(x, shared, y_tok, gate, ln_g.reshape(1, D), ln_b.reshape(1, D))


MLA_QK_PAD = 2 * MLA_NOPE_DIM
MLA_IN_PAD = MLA_Q_RANK + MLA_KV_RANK + 2 * MLA_NOPE_DIM


def _mla_proj_kernel(x_ref, win_ref, wq_ref, wkv_ref, qg_ref, kvg_ref, csq_ref, ck_ref, sk_ref,
                     q_ref, k_ref, v_ref):
    NP = MLA_NOPE_DIM
    c = jnp.dot(x_ref[...], win_ref[...], preferred_element_type=F32)

    def rms(t, g):
        return (t * lax.rsqrt(jnp.mean(t * t, axis=-1, keepdims=True) + RMS_EPS) * g).astype(BF16)

    cq = rms(c[:, :MLA_Q_RANK], qg_ref[...])
    ckv = rms(c[:, MLA_Q_RANK:MLA_Q_RANK + MLA_KV_RANK], kvg_ref[...])
    o = MLA_Q_RANK + MLA_KV_RANK
    k_rope = (c[:, o:o + NP] * ck_ref[...] + c[:, o + NP:o + 2 * NP] * sk_ref[...]).astype(BF16)
    low = lax.broadcasted_iota(jnp.int32, (x_ref.shape[0], NP), 1) < MLA_ROPE_DIM
    csq = csq_ref[...]
    for h in range(MLA_HEADS):
        qh = jnp.dot(cq, wq_ref[:, h * 2 * NP:(h + 1) * 2 * NP], preferred_element_type=F32)
        u = qh[:, NP:] * csq
        q_rope = jnp.where(low, u + pltpu.roll(u, MLA_ROPE_DIM, axis=1), 0.0)
        q_ref[0, h, :, :NP] = (qh[:, :NP] * MLA_Q_SCALE).astype(BF16)
        q_ref[0, h, :, NP:] = (q_rope * MLA_Q_SCALE).astype(BF16)
        kvh = jnp.dot(ckv, wkv_ref[:, h * 2 * NP:(h + 1) * 2 * NP], preferred_element_type=F32)
        k_ref[0, h, :, :NP] = kvh[:, :NP].astype(BF16)
        k_ref[0, h, :, NP:] = k_rope
        v_ref[0, h] = kvh[:, NP:].astype(BF16)


def _mla_proj(x16, w_in_ext, w_q_ext, w_kvb, q_norm_g, kv_norm_g, csq, ck, sk, batch, tm=256):
    T, D = x16.shape
    S = T // batch
    H = MLA_HEADS
    tm = min(tm, S)
    assert S % tm == 0
    nsb = S // tm
    w_bytes = (w_in_ext.size + w_q_ext.size + w_kvb.size) * 2
    out_bytes = _nbytes((H, tm, 2 * MLA_QK_PAD + MLA_V_DIM), BF16)
    need = w_bytes + 2 * _nbytes((tm, D), BF16) + 2 * out_bytes + 4 * _nbytes((tm, MLA_IN_PAD), F32)

    def const(shape):
        return pl.BlockSpec(shape, lambda i: (0, 0), pipeline_mode=pl.Buffered(1))

    def rows(width):
        return pl.BlockSpec((tm, width), lambda i: (i, 0))

    def heads(width):
        return pl.BlockSpec((1, H, tm, width), lambda i: (i // nsb, 0, i % nsb, 0))

    return pl.pallas_call(
        _mla_proj_kernel,
        grid=(T // tm,),
        in_specs=[rows(D), const(w_in_ext.shape), const(w_q_ext.shape), const(w_kvb.shape),
                  const((1, MLA_Q_RANK)), const((1, MLA_KV_RANK)),
                  rows(MLA_NOPE_DIM), rows(MLA_NOPE_DIM), rows(MLA_NOPE_DIM)],
        out_specs=[heads(MLA_QK_PAD), heads(MLA_QK_PAD), heads(MLA_V_DIM)],
        out_shape=[jax.ShapeDtypeStruct((batch, H, S, MLA_QK_PAD), BF16),
                   jax.ShapeDtypeStruct((batch, H, S, MLA_QK_PAD), BF16),
                   jax.ShapeDtypeStruct((batch, H, S, MLA_V_DIM), BF16)],
        compiler_params=pltpu.CompilerParams(
            dimension_semantics=("parallel",),
            vmem_limit_bytes=_vmem_limit(need + (6 << 20))),
        name="mla_proj",
    )(x16, w_in_ext, w_q_ext, w_kvb, q_norm_g.reshape(1, -1), kv_norm_g.reshape(1, -1), csq, ck, sk)


def _router_kernel(x_ref, wt_ref, bias_ref, e_ref, gate_ref, rank_ref, cnt_ref, carry_ref):
    E, G, K = N_EXPERTS, N_GROUPS, TOP_K
    P = E // G
    tm = x_ref.shape[0]

    @pl.when(pl.program_id(0) == 0)
    def _():
        carry_ref[...] = jnp.zeros_like(carry_ref)

    logits = lax.dot_general(wt_ref[...], x_ref[...], (((1,), (1,)), ((), ())),
                             precision=lax.Precision.HIGHEST, preferred_element_type=F32)
    scores = jax.nn.sigmoid(logits)
    biased = scores + bias_ref[...]

    def preceding(v):
        n = v.shape[0]
        idx = lax.broadcasted_iota(jnp.int32, (n, 1), 0)
        cnt = jnp.zeros(v.shape, F32)
        for j in range(n):
            row = v[j:j + 1, :]
            tie = jnp.where(idx > j, 1.0, 0.0)
            cnt = cnt + jnp.where(row > v, 1.0, jnp.where(row == v, tie, 0.0))
        return cnt

    blocks = [biased[g * P:(g + 1) * P, :] for g in range(G)]
    group_score = jnp.concatenate(
        [jnp.sum(jnp.where(preceding(b) < 2.0, b, 0.0), axis=0, keepdims=True) for b in blocks], axis=0)
    group_sel = preceding(group_score) < float(TOPK_GROUPS)
    masked = jnp.concatenate(
        [jnp.where(group_sel[g:g + 1, :], blocks[g], -jnp.inf) for g in range(G)], axis=0)
    sel = preceding(masked) < float(K)
    sel_f = jnp.where(sel, 1.0, 0.0)
    gate = jnp.where(sel, scores, 0.0)
    gate = gate / (jnp.sum(gate, axis=0, keepdims=True) + 1e-20) * ROUTED_SCALE

    t_row = lax.broadcasted_iota(jnp.int32, (tm, tm), 0)
    t_col = lax.broadcasted_iota(jnp.int32, (tm, tm), 1)
    before_t = jnp.where(t_row < t_col, 1.0, 0.0).astype(BF16)
    sel16 = sel_f.astype(BF16)
    rank = jnp.dot(sel16, before_t, preferred_element_type=F32) + carry_ref[...]
    carry_ref[...] = carry_ref[...] + jnp.sum(sel_f, axis=1, keepdims=True)
    e_row = lax.broadcasted_iota(jnp.int32, (E, E), 0)
    e_col = lax.broadcasted_iota(jnp.int32, (E, E), 1)
    before_e = jnp.where(e_col < e_row, 1.0, 0.0).astype(BF16)
    slot = jnp.where(sel, jnp.dot(before_e, sel16, preferred_element_type=F32), -1.0)
    e_idx = lax.broadcasted_iota(jnp.int32, (E, 1), 0).astype(F32)

    def pick(k, t):
        return jnp.sum(jnp.where(slot == float(k), t, 0.0), axis=0, keepdims=True)

    e_ref[...] = jnp.concatenate([pick(k, e_idx) for k in range(K)], axis=0).astype(jnp.int32)
    rank_ref[...] = jnp.concatenate([pick(k, rank) for k in range(K)], axis=0).astype(jnp.int32)
    gate_ref[...] = jnp.concatenate([pick(k, gate) for k in range(K)], axis=0)
    cnt_ref[...] = jnp.broadcast_to(carry_ref[...], cnt_ref.shape).astype(jnp.int32)


def _router(x, router_wt, router_bias, tm=512):
    T, D = x.shape
    E = router_wt.shape[0]
    tm = min(tm, T)
    assert T % tm == 0
    kt = pl.BlockSpec((TOP_K, tm), lambda i: (0, i))
    top_e, gate, rank, cnt = pl.pallas_call(
        _router_kernel,
        grid=(T // tm,),
        in_specs=[pl.BlockSpec((tm, D), lambda i: (i, 0)),
                  pl.BlockSpec((E, D), lambda i: (0, 0)),
                  pl.BlockSpec((E, 1), lambda i: (0, 0))],
        out_specs=[kt, kt, kt, pl.BlockSpec((E, 128), lambda i: (0, 0))],
        out_shape=[jax.ShapeDtypeStruct((TOP_K, T), jnp.int32), jax.ShapeDtypeStruct((TOP_K, T), F32),
                   jax.ShapeDtypeStruct((TOP_K, T), jnp.int32), jax.ShapeDtypeStruct((E, 128), jnp.int32)],
        scratch_shapes=[pltpu.VMEM((E, 1), F32)],
        compiler_params=pltpu.CompilerParams(
            dimension_semantics=("arbitrary",),
            vmem_limit_bytes=_vmem_limit(2 * _nbytes((tm, D), F32) + (16 << 20))),
        name="moe_router",
    )(x, router_wt, router_bias.reshape(E, 1))
    return top_e, gate, rank, cnt[:, 0]


def _mla_mixer(x, x16, positions, w_in, q_norm_g, kv_norm_g, w_qb, w_kvb, w_o, ln_g, ln_b):
    B, S = positions.shape
    T, D = x.shape
    H = MLA_HEADS
    half = MLA_ROPE_DIM // 2

    def rot(w):
        return jnp.concatenate([-w[..., half:], w[..., :half]], axis=-1)

    lat = MLA_Q_RANK + MLA_KV_RANK
    w_kr = w_in[:, lat:]
    zpad = jnp.zeros((D, MLA_NOPE_DIM - MLA_ROPE_DIM), w_in.dtype)
    w_in_ext = jnp.concatenate([w_in[:, :lat], w_kr, zpad, rot(w_kr), zpad], axis=1).astype(BF16)
    w_q3 = w_qb.reshape(MLA_Q_RANK, H, MLA_QK_DIM)
    w_q_rope = w_q3[..., MLA_NOPE_DIM:]
    w_q_ext = jnp.concatenate([w_q3[..., :MLA_NOPE_DIM], w_q_rope, rot(w_q_rope)], axis=-1)
    w_q_ext = w_q_ext.reshape(MLA_Q_RANK, H * MLA_QK_PAD).astype(BF16)
    inv_freq = 1.0 / (ROPE_THETA ** (jnp.arange(0, MLA_ROPE_DIM, 2, dtype=F32) / MLA_ROPE_DIM))
    ang = positions.astype(F32).reshape(T, 1) * inv_freq
    cos, sin = jnp.cos(ang), jnp.sin(ang)
    zero = jnp.zeros_like(cos)
    csq = jnp.concatenate([cos, cos, sin, sin], axis=1)
    ck = jnp.concatenate([cos, cos, zero, zero], axis=1)
    sk = jnp.concatenate([sin, sin, zero, zero], axis=1)
    q_cat, k_cat, v = _mla_proj(x16, w_in_ext, w_q_ext, w_kvb.astype(BF16), q_norm_g, kv_norm_g,
                                csq, ck, sk, B)
    o = _attention(q_cat, k_cat, v)
    return _matmul_ln(o.reshape(T, H * MLA_V_DIM), w_o.astype(BF16), x, ln_g, ln_b, 512, "mla_out_ln")


def _gdn_mixer(x, x16, batch, w_in, conv_w, a_log, dt_bias, norm_g, w_out, ln_g, ln_b):
    T, D = x.shape
    B, S = batch, T // batch
    n_main = GDN_CONV_DIM + GDN_V_WIDTH
    p = _matmul(x16, w_in[:, :n_main].astype(BF16), F32, name="gdn_in")
    ba = _matmul(x16, w_in[:, n_main:].astype(BF16), F32, name="gdn_in_ba")
    b = ba[:, :GDN_V_HEADS].reshape(B, S, GDN_V_HEADS)
    a = ba[:, GDN_V_HEADS:].reshape(B, S, GDN_V_HEADS)
    beta = jax.nn.sigmoid(b)
    g = -jnp.exp(a_log) * jax.nn.softplus(a + dt_bias)
    N = S // CHUNK
    g_cum = jnp.cumsum(g.reshape(B, N, CHUNK, GDN_V_HEADS), axis=2)
    g_cum = jnp.transpose(g_cum, (0, 3, 1, 2))
    beta = jnp.transpose(beta.reshape(B, N, CHUNK, GDN_V_HEADS), (0, 3, 1, 2))
    o = _gated_delta_rule(p.reshape(B, S, n_main), conv_w, g_cum, beta, norm_g.reshape(1, GDN_V_DIM))
    return _matmul_ln(o.reshape(T, GDN_V_WIDTH), w_out.astype(BF16), x, ln_g, ln_b, 256, "gdn_out_ln")


def _moe_ffn(x, xpk, layer, router_w, router_bias, w_gate_up, w_down, shared_gate_up, shared_down, ln_g, ln_b):
    T, D = x.shape
    E, K = N_EXPERTS, TOP_K
    top_e, gate, rank, counts = _router(x, router_w.T, router_bias)

    zero = jnp.zeros((1,), jnp.int32)
    tm_sh = min(512, T)
    shared = _grouped_swiglu(xpk, shared_gate_up, shared_down, layer,
                             jnp.zeros((T // tm_sh,), jnp.int32), zero + T // tm_sh, tm_sh, F32,
                             "shared_expert")

    n_assign = T * K
    n_blocks = -(-n_assign // MOE_BLOCK) + E
    starts = jnp.cumsum(counts) - counts
    padded = (counts + MOE_BLOCK - 1) // MOE_BLOCK * MOE_BLOCK
    pad_ends = jnp.cumsum(padded)
    pad_starts = pad_ends - padded
    e_ids = jnp.arange(E, dtype=jnp.int32)
    pos = jnp.sum(jnp.where(top_e[..., None] == e_ids, pad_starts, 0), axis=-1) + rank
    blk_first = jnp.arange(n_blocks, dtype=jnp.int32) * MOE_BLOCK
    blk_expert = jnp.minimum(jnp.sum((pad_ends[None, :] <= blk_first[:, None]).astype(jnp.int32), axis=1),
                             E - 1)
    n_used = (pad_ends[-1:] // MOE_BLOCK).astype(jnp.int32)
    tok = jnp.arange(T, dtype=jnp.int32)
    order = jnp.sort((top_e * T + tok[None, :]).reshape(n_assign)) % T
    row_j = (blk_first - pad_starts[blk_expert])[:, None] + jnp.arange(MOE_BLOCK, dtype=jnp.int32)[None, :]
    row_src = jnp.clip(starts[blk_expert][:, None] + row_j, 0, n_assign - 1)
    pad_tok = (blk_first[:, None] + jnp.arange(MOE_BLOCK, dtype=jnp.int32)[None, :]) % T
    row_tok = jnp.where(row_j < counts[blk_expert][:, None], order[row_src], pad_tok)

    n_chunks = MOE_GATHER_CHUNKS if n_blocks % MOE_GATHER_CHUNKS == 0 else 1
    cb = n_blocks // n_chunks
    x_rows = [xpk.at[row_tok[c * cb:(c + 1) * cb].reshape(-1)].get(mode="promise_in_bounds")
              for c in range(n_chunks)]
    y_rows = _zero_rows(n_blocks * MOE_BLOCK, D // 2, jnp.uint32, 16 * MOE_BLOCK)
    for c in range(n_chunks):
        y_rows = _grouped_swiglu(x_rows[c], w_gate_up, w_down, layer, blk_expert[c * cb:(c + 1) * cb],
                                 jnp.clip(n_used - c * cb, 0, cb), MOE_BLOCK, jnp.uint32, "routed_experts",
                                 out_prev=y_rows, out_block0=c * cb)
    y_tok = y_rows.at[pos.reshape(n_assign)].get(mode="promise_in_bounds").reshape(K, T, D // 2)
    return _moe_combine(x, shared, y_tok, gate.T, ln_g, ln_b)


def kernel(x, positions, mla_w_in, mla_q_norm, mla_kv_norm, mla_w_qb, mla_w_kvb, mla_w_o, gdn_w_in, gdn_conv_w, gdn_a_log, gdn_dt_bias, gdn_norm, gdn_w_out, ln_mix_g, ln_mix_b, ln_ffn_g, ln_ffn_b, moe_router, moe_router_bias, moe_w_gate_up, moe_w_down, moe_shared_gate_up, moe_shared_down):
    B, S, D = x.shape
    x = x.reshape(B * S, D)
    x16 = x.astype(BF16)
    for i in range(DEPTH):
        j = i // 2
        if i % 2 == 0:
            x, xpk = _mla_mixer(x, x16, positions, mla_w_in[j], mla_q_norm[j], mla_kv_norm[j],
                                mla_w_qb[j], mla_w_kvb[j], mla_w_o[j], ln_mix_g[i], ln_mix_b[i])
        else:
            x, xpk = _gdn_mixer(x, x16, B, gdn_w_in[j], gdn_conv_w[j], gdn_a_log[j], gdn_dt_bias[j],
                                gdn_norm[j], gdn_w_out[j], ln_mix_g[i], ln_mix_b[i])
        x, x16 = _moe_ffn(x, xpk, i, moe_router[i], moe_router_bias[i], moe_w_gate_up, moe_w_down,
                          moe_shared_gate_up[:, None], moe_shared_down[:, None], ln_ffn_g[i], ln_ffn_b[i])
    return x.reshape(B, S, D)
```

```python
import functools
import math

import jax
import jax.numpy as jnp
from jax import lax
from jax.experimental import pallas as pl
from jax.experimental.pallas import tpu as pltpu

F32 = jnp.float32
BF16 = jnp.bfloat16

DEPTH = 2
MLA_HEADS = 16
MLA_Q_RANK = 512
MLA_KV_RANK = 512
MLA_NOPE_DIM = 128
MLA_ROPE_DIM = 64
MLA_V_DIM = 128
MLA_QK_DIM = MLA_NOPE_DIM + MLA_ROPE_DIM
MLA_SCALE = MLA_QK_DIM ** -0.5
MLA_Q_SCALE = MLA_SCALE * math.log2(math.e)
ROPE_THETA = 10000.0

GDN_K_HEADS = 16
GDN_V_HEADS = 32
GDN_K_DIM = 128
GDN_V_DIM = 128
CONV_K = 4
CHUNK = 64
GDN_QK_WIDTH = GDN_K_HEADS * GDN_K_DIM
GDN_V_WIDTH = GDN_V_HEADS * GDN_V_DIM
GDN_CONV_DIM = 2 * GDN_QK_WIDTH + GDN_V_WIDTH

N_EXPERTS = 64
TOP_K = 8
N_GROUPS = 8
TOPK_GROUPS = 4
EXPERT_DIM = 512
ROUTED_SCALE = 2.5
MOE_BLOCK = 256
MOE_GATHER_CHUNKS = 4

ALPHA = (2 * DEPTH) ** 0.25
LN_EPS = 1e-5
RMS_EPS = 1e-6

V7X_VMEM_BYTES = 64 * 1024 * 1024
VMEM_LIMIT_CAP = 56 * 1024 * 1024
NEG_BIG = -1e30


def _vmem_limit(nbytes):
    return int(min(max(nbytes, 16 * 1024 * 1024), VMEM_LIMIT_CAP))


def _nbytes(shape, dtype):
    return math.prod(shape) * jnp.dtype(dtype).itemsize


def _layer_norm_rows(y, g, b):
    mu = jnp.mean(y, axis=-1, keepdims=True)
    yc = y - mu
    var = jnp.mean(yc * yc, axis=-1, keepdims=True)
    return yc * lax.rsqrt(var + LN_EPS) * g + b


def _pack_halves(x):
    n = x.shape[-1] // 2
    lo = pltpu.bitcast(x[:, :n].astype(BF16).astype(F32), jnp.uint32)
    hi = pltpu.bitcast(x[:, n:].astype(BF16).astype(F32), jnp.uint32)
    return hi | (lo >> 16)


def _unpack_halves(w):
    lo = pltpu.bitcast(w << 16, F32)
    hi = pltpu.bitcast(w & jnp.uint32(0xFFFF0000), F32)
    return lo, hi


def _mm_kernel(x_ref, w_ref, o_ref):
    o_ref[...] = jnp.dot(x_ref[...], w_ref[...], preferred_element_type=F32).astype(o_ref.dtype)


def _matmul(x, w, out_dtype, tm=1024, tn=1024, name="matmul"):
    M, K = x.shape
    N = w.shape[1]
    tm = min(tm, M)
    tn = min(tn, N)
    assert M % tm == 0 and N % tn == 0, (M, N, tm, tn)
    need = 2 * (_nbytes((tm, K), x.dtype) + _nbytes((K, tn), w.dtype) + _nbytes((tm, tn), out_dtype))
    need += _nbytes((tm, tn), F32)
    return pl.pallas_call(
        _mm_kernel,
        grid=(N // tn, M // tm),
        in_specs=[pl.BlockSpec((tm, K), lambda j, i: (i, 0)),
                  pl.BlockSpec((K, tn), lambda j, i: (0, j))],
        out_specs=pl.BlockSpec((tm, tn), lambda j, i: (i, j)),
        out_shape=jax.ShapeDtypeStruct((M, N), out_dtype),
        compiler_params=pltpu.CompilerParams(
            dimension_semantics=("parallel", "parallel"),
            vmem_limit_bytes=_vmem_limit(need + (4 << 20))),
        name=name,
    )(x, w)


def _mm_ln_kernel(x_ref, w_ref, r_ref, g_ref, b_ref, o_ref, opk_ref):
    y = ALPHA * r_ref[...] + jnp.dot(x_ref[...], w_ref[...], preferred_element_type=F32)
    out = _layer_norm_rows(y, g_ref[...], b_ref[...])
    o_ref[...] = out
    opk_ref[...] = _pack_halves(out)


def _matmul_ln(x, w, resid, ln_g, ln_b, tm, name):
    M, K = x.shape
    N = w.shape[1]
    tm = min(tm, M)
    assert M % tm == 0
    need = (_nbytes((K, N), BF16) + 2 * _nbytes((tm, K), BF16) + 4 * _nbytes((tm, N), F32)
            + 2 * _nbytes((tm, N), BF16) + 2 * _nbytes((tm, N), F32))
    return pl.pallas_call(
        _mm_ln_kernel,
        grid=(M // tm,),
        in_specs=[pl.BlockSpec((tm, K), lambda i: (i, 0)),
                  pl.BlockSpec((K, N), lambda i: (0, 0), pipeline_mode=pl.Buffered(1)),
                  pl.BlockSpec((tm, N), lambda i: (i, 0)),
                  pl.BlockSpec((1, N), lambda i: (0, 0)),
                  pl.BlockSpec((1, N), lambda i: (0, 0))],
        out_specs=[pl.BlockSpec((tm, N), lambda i: (i, 0)),
                   pl.BlockSpec((tm, N // 2), lambda i: (i, 0))],
        out_shape=[jax.ShapeDtypeStruct((M, N), F32), jax.ShapeDtypeStruct((M, N // 2), jnp.uint32)],
        compiler_params=pltpu.CompilerParams(
            dimension_semantics=("parallel",),
            vmem_limit_bytes=_vmem_limit(need + (4 << 20))),
        name=name,
    )(x, w, resid, ln_g.reshape(1, N), ln_b.reshape(1, N))


def _attn_kernel(q_ref, k_ref, v_ref, o_ref, *, blk, wide):
    i = pl.program_id(2)
    q = q_ref[0, 0]

    def step(carry, off, width, mask):
        m, l, acc = carry
        k = k_ref[0, 0, pl.ds(off, width), :]
        s = lax.dot_general(q, k, (((1,), (1,)), ((), ())), preferred_element_type=F32)
        if mask is not None:
            s = jnp.where(mask, s, NEG_BIG)
        m_new = jnp.maximum(m, jnp.max(s, axis=-1, keepdims=True))
        alpha = jnp.exp2(m - m_new)
        p = jnp.exp2(s - m_new)
        l = alpha * l + jnp.sum(p, axis=-1, keepdims=True)
        v = v_ref[0, 0, pl.ds(off, width), :]
        acc = alpha * acc + jnp.dot(p.astype(BF16), v, preferred_element_type=F32)
        return m_new, l, acc

    carry = (jnp.full((blk, 1), NEG_BIG, F32), jnp.zeros((blk, 1), F32), jnp.zeros((blk, MLA_V_DIM), F32))
    n_wide = i // wide
    carry = lax.fori_loop(
        0, n_wide, lambda j, c: step(c, pl.multiple_of(j * (wide * blk), wide * blk), wide * blk, None), carry)
    carry = lax.fori_loop(
        n_wide * wide, i, lambda j, c: step(c, pl.multiple_of(j * blk, blk), blk, None), carry)
    row = lax.broadcasted_iota(jnp.int32, (blk, blk), 0)
    col = lax.broadcasted_iota(jnp.int32, (blk, blk), 1)
    m, l, acc = step(carry, pl.multiple_of(i * blk, blk), blk, row >= col)
    o_ref[0] = (acc / l).astype(o_ref.dtype)


def _attention(q, k, v, blk=1024, wide=1):
    B, H, S, qk_w = q.shape
    blk = min(blk, S)
    assert S % blk == 0
    need = 2 * (_nbytes((blk, 256), BF16) + _nbytes((S, 256), BF16) + _nbytes((S, 128), BF16)
                + _nbytes((blk, 128), BF16)) + 6 * _nbytes((blk, blk), F32)
    return pl.pallas_call(
        functools.partial(_attn_kernel, blk=blk, wide=wide),
        grid=(B, H, S // blk),
        in_specs=[pl.BlockSpec((1, 1, blk, qk_w), lambda b, h, i: (b, h, i, 0)),
                  pl.BlockSpec((1, 1, S, qk_w), lambda b, h, i: (b, h, 0, 0)),
                  pl.BlockSpec((1, 1, S, MLA_V_DIM), lambda b, h, i: (b, h, 0, 0))],
        out_specs=pl.BlockSpec((1, blk, MLA_V_DIM), lambda b, h, i: (b, i, h)),
        out_shape=jax.ShapeDtypeStruct((B, S, H * MLA_V_DIM), BF16),
        compiler_params=pltpu.CompilerParams(
            dimension_semantics=("parallel", "parallel", "parallel"),
            vmem_limit_bytes=_vmem_limit(need + (8 << 20))),
        name="mla_attention",
    )(q, k, v)


def _gdn_kernel(pq_ref, pk_ref, pv_ref, pz_ref, hq_ref, hk_ref, hv_ref, cq_ref, ck_ref, cv_ref,
                g_ref, b_ref, ng_ref, o_ref, state_ref, *, n_chunks, k_heads):
    C = CHUNK
    rep = GDN_V_HEADS // GDN_K_HEADS
    v_heads = k_heads * rep
    DK, DV = GDN_K_DIM, GDN_V_DIM
    first_block = pl.program_id(2) == 0

    @pl.when(first_block)
    def _():
        state_ref[...] = jnp.zeros_like(state_ref)

    row = lax.broadcasted_iota(jnp.int32, (C, C), 0)
    col = lax.broadcasted_iota(jnp.int32, (C, C), 1)
    causal = row >= col
    strict = row > col
    eye = row == col
    eye_f = eye.astype(F32)
    nt = (((1,), (1,)), ((), ()))
    tn = (((0,), (0,)), ((), ()))

    def dot(a, b):
        return jnp.dot(a, b, preferred_element_type=F32)

    def conv_silu(x_ref, halo_ref, w_ref, c, r0):
        cur = x_ref[0, pl.ds(r0, C), :]
        prev_start = pl.multiple_of(jnp.maximum(c * (C // 8) - 1, 0) * 8, 8)
        prev_rows = x_ref[0, pl.ds(prev_start, 8), :]
        halo = jnp.where(first_block, 0.0, halo_ref[0])
        ext = jnp.concatenate([jnp.where(c == 0, halo, prev_rows), cur], axis=0)
        acc = cur * w_ref[CONV_K - 1:CONV_K, :]
        for j in range(1, CONV_K):
            acc = acc + pltpu.roll(ext, j, axis=0)[8:] * w_ref[CONV_K - 1 - j:CONV_K - j, :]
        return acc * jax.nn.sigmoid(acc)

    def l2n(t):
        return t * lax.rsqrt(jnp.sum(t * t, axis=-1, keepdims=True) + RMS_EPS)

    def chunk(c, carry):
        r0 = pl.multiple_of(c * C, C)
        qa = conv_silu(pq_ref, hq_ref, cq_ref, c, r0)
        ka = conv_silu(pk_ref, hk_ref, ck_ref, c, r0)
        va = conv_silu(pv_ref, hv_ref, cv_ref, c, r0)
        q = [l2n(qa[:, i * DK:(i + 1) * DK]) * (DK ** -0.5) for i in range(k_heads)]
        k = [l2n(ka[:, i * DK:(i + 1) * DK]) for i in range(k_heads)]
        v = [va[:, h * DV:(h + 1) * DV] for h in range(v_heads)]
        k16 = [t.astype(BF16) for t in k]
        qk16 = [jnp.concatenate([q[i], k[i]], axis=0).astype(BF16) for i in range(k_heads)]
        qkk = [lax.dot_general(qk16[i], k16[i], nt, preferred_element_type=F32) for i in range(k_heads)]

        g_row = [g_ref[0, h, pl.ds(c, 1), :] for h in range(v_heads)]
        b_row = [b_ref[0, h, pl.ds(c, 1), :] for h in range(v_heads)]
        g_col = [jnp.sum(jnp.where(eye, t, 0.0), axis=1, keepdims=True) for t in g_row]
        b_col = [jnp.sum(jnp.where(eye, t, 0.0), axis=1, keepdims=True) for t in b_row]
        g_last = [t[:, C - 1:C] for t in g_row]
        decay = [jnp.exp(jnp.where(causal, g_col[h] - g_row[h], NEG_BIG)) for h in range(v_heads)]
        a = [jnp.where(strict, qkk[h // rep][C:] * b_col[h] * decay[h], 0.0) for h in range(v_heads)]
        qkd = [(qkk[h // rep][:C] * decay[h]).astype(BF16) for h in range(v_heads)]

        a16 = [t.astype(BF16) for t in a]
        x = [dot(t, t) for t in a16]
        p = [eye_f - t for t in a]
        for _ in range(int(math.log2(C)) - 2):
            x16 = [t.astype(BF16) for t in x]
            px = [dot(jnp.concatenate([p[h].astype(BF16), x16[h]], axis=0), x16[h]) for h in range(v_heads)]
            p = [p[h] + px[h][:C] for h in range(v_heads)]
            x = [px[h][C:] for h in range(v_heads)]
        p = [p[h] + dot(p[h].astype(BF16), x[h].astype(BF16)) for h in range(v_heads)]

        eg = [jnp.exp(t) for t in g_col]
        rhs = [jnp.concatenate([v[h] * b_col[h], k[h // rep] * (b_col[h] * eg[h])], axis=1).astype(BF16)
               for h in range(v_heads)]
        sol = [dot(p[h].astype(BF16), rhs[h]) for h in range(v_heads)]
        s = [state_ref[h] for h in range(v_heads)]
        wq = [jnp.concatenate([sol[h][:, DV:], q[h // rep] * eg[h]], axis=0).astype(BF16)
              for h in range(v_heads)]
        ws = [dot(wq[h], s[h].astype(BF16)) for h in range(v_heads)]
        u16 = [(sol[h][:, :DV] - ws[h][:C]).astype(BF16) for h in range(v_heads)]
        k_dec = [(k[h // rep] * jnp.exp(g_last[h] - g_col[h])).astype(BF16) for h in range(v_heads)]
        ds = [lax.dot_general(k_dec[h], u16[h], tn, preferred_element_type=F32) for h in range(v_heads)]
        o = [ws[h][C:] + dot(qkd[h], u16[h]) for h in range(v_heads)]
        for h in range(v_heads):
            state_ref[h] = s[h] * jnp.exp(g_last[h]) + ds[h]
            z = pz_ref[0, pl.ds(r0, C), h * DV:(h + 1) * DV]
            on = o[h] * lax.rsqrt(jnp.mean(o[h] * o[h], axis=-1, keepdims=True) + RMS_EPS) * ng_ref[...]
            o_ref[0, pl.ds(r0, C), h * DV:(h + 1) * DV] = (on * (z * jax.nn.sigmoid(z))).astype(o_ref.dtype)
        return carry

    lax.fori_loop(0, n_chunks, chunk, 0)


def _gated_delta_rule(p, conv_w, g_cum, beta, norm_g, n_chunks=8, k_heads=8):
    B, S, _ = p.shape
    n_chunks = min(n_chunks, S // CHUNK)
    sb = n_chunks * CHUNK
    rep = GDN_V_HEADS // GDN_K_HEADS
    v_heads = k_heads * rep
    wq, wv = k_heads * GDN_K_DIM, v_heads * GDN_V_DIM
    assert S % sb == 0 and GDN_K_HEADS % k_heads == 0
    k_off, v_off, z_off = GDN_QK_WIDTH // wq, 2 * GDN_QK_WIDTH // wv, GDN_CONV_DIM // wv
    hb = sb // 8

    def main(width, off):
        return pl.BlockSpec((1, sb, width), lambda b, h, s: (b, s, off + h))

    def halo(width, off):
        return pl.BlockSpec((1, 8, width), lambda b, h, s: (b, jnp.maximum(s * hb - 1, 0), off + h))

    def cw(width, off):
        return pl.BlockSpec((CONV_K, width), lambda b, h, s: (0, off + h))

    blocks = 2 * sb * (2 * wq + 2 * wv) * 4 + 2 * sb * wv * 2
    return pl.pallas_call(
        functools.partial(_gdn_kernel, n_chunks=n_chunks, k_heads=k_heads),
        grid=(B, GDN_K_HEADS // k_heads, S // sb),
        in_specs=[main(wq, 0), main(wq, k_off), main(wv, v_off), main(wv, z_off),
                  halo(wq, 0), halo(wq, k_off), halo(wv, v_off),
                  cw(wq, 0), cw(wq, k_off), cw(wv, v_off),
                  pl.BlockSpec((1, v_heads, n_chunks, CHUNK), lambda b, h, s: (b, h, s, 0)),
                  pl.BlockSpec((1, v_heads, n_chunks, CHUNK), lambda b, h, s: (b, h, s, 0)),
                  pl.BlockSpec((1, GDN_V_DIM), lambda b, h, s: (0, 0))],
        out_specs=pl.BlockSpec((1, sb, wv), lambda b, h, s: (b, s, h)),
        out_shape=jax.ShapeDtypeStruct((B, S, GDN_V_WIDTH), BF16),
        scratch_shapes=[pltpu.VMEM((v_heads, GDN_K_DIM, GDN_V_DIM), F32)],
        compiler_params=pltpu.CompilerParams(
            dimension_semantics=("parallel", "parallel", "arbitrary"),
            vmem_limit_bytes=_vmem_limit(blocks + (12 << 20))),
        name="gated_delta_rule",
    )(p, p, p, p, p, p, p, conv_w, conv_w, conv_w, g_cum, beta, norm_g)


def _expert_kernel(be_ref, nx_ref, nu_ref, x_ref, wgu_hbm, wd_hbm, *rest, layer):
    o_ref, wgu32_ref, wd32_ref, wgu16_ref, wd16_ref, sem_ref, run_ref = rest[-7:]
    i = pl.program_id(0)
    used = i < nu_ref[0]
    new_expert = jnp.logical_or(i == 0, be_ref[i] != be_ref[jnp.maximum(i - 1, 0)])

    def weight_copies(expert, slot):
        return (pltpu.make_async_copy(wgu_hbm.at[layer, expert], wgu32_ref.at[slot], sem_ref.at[0, slot]),
                pltpu.make_async_copy(wd_hbm.at[layer, expert], wd32_ref.at[slot], sem_ref.at[1, slot]))

    @pl.when(i == 0)
    def _():
        run_ref[0] = 0

    @pl.when(jnp.logical_and(used, i == 0))
    def _():
        for cp in weight_copies(be_ref[0], 0):
            cp.start()

    @pl.when(jnp.logical_and(used, new_expert))
    def _():
        slot = run_ref[0] % 2
        for cp in weight_copies(be_ref[i], slot):
            cp.wait()
        wgu16_ref[...] = wgu32_ref[slot].astype(BF16)
        wd16_ref[...] = wd32_ref[slot].astype(BF16)

        @pl.when(nx_ref[i] >= 0)
        def _():
            for cp in weight_copies(nx_ref[i], 1 - slot):
                cp.start()

        run_ref[0] = run_ref[0] + 1

    @pl.when(used)
    def _():
        x_lo, x_hi = _unpack_halves(x_ref[...])
        half = x_lo.shape[-1]
        gu = (jnp.dot(x_lo.astype(BF16), wgu16_ref[:half, :], preferred_element_type=F32)
              + jnp.dot(x_hi.astype(BF16), wgu16_ref[half:, :], preferred_element_type=F32))
        f = gu.shape[-1] // 2
        gate = gu[:, :f]
        act = gate * jax.nn.sigmoid(gate) * gu[:, f:]
        y = jnp.dot(act.astype(BF16), wd16_ref[...], preferred_element_type=F32)
        if o_ref.dtype == jnp.uint32:
            o_ref[...] = _pack_halves(y)
        else:
            o_ref[...] = y.astype(o_ref.dtype)

    @pl.when(jnp.logical_not(used))
    def _():
        o_ref[...] = jnp.zeros_like(o_ref)


def _grouped_swiglu(x_rows, w_gate_up, w_down, layer, blk_expert, n_used, tm, out_dtype, name,
                    out_prev=None, out_block0=0, total_blocks=None):
    R = x_rows.shape[0]
    D = w_gate_up.shape[-2]
    F2 = w_gate_up.shape[-1]
    tm = min(tm, R)
    assert R % tm == 0 and x_rows.shape[1] * 2 == D
    nb = R // tm
    grid_blocks = nb if total_blocks is None else total_blocks
    d_out = D // 2 if out_dtype == jnp.uint32 else D
    w_elems = D * F2 + (F2 // 2) * D
    need = (2 * w_elems * 4 + w_elems * 2 + 2 * _nbytes((tm, D), BF16) + 2 * _nbytes((tm, D), F32)
            + 2 * _nbytes((tm, F2), F32) + _nbytes((tm, D), F32))
    idx = jnp.arange(nb, dtype=jnp.int32)
    is_first = jnp.concatenate([jnp.ones((1,), bool), blk_expert[1:] != blk_expert[:-1]])
    next_first = lax.cummin(jnp.where(is_first, idx, nb)[::-1])[::-1]
    next_first = jnp.concatenate([next_first[1:], jnp.full((1,), nb, jnp.int32)])
    has_next = next_first < jnp.minimum(n_used[0], nb)
    nxt = jnp.where(has_next, blk_expert[jnp.minimum(next_first, nb - 1)], -1).astype(jnp.int32)
    if grid_blocks > nb:
        pad = grid_blocks - nb
        blk_expert = jnp.concatenate([blk_expert, jnp.broadcast_to(blk_expert[-1:], (pad,))])
        nxt = jnp.concatenate([nxt, jnp.full((pad,), -1, jnp.int32)])
    n_used = jnp.minimum(n_used, nb)
    in_specs = [pl.BlockSpec((tm, D // 2), lambda i, be, nx, nu: (jnp.minimum(i, nb - 1), 0)),
                pl.BlockSpec(memory_space=pl.ANY),
                pl.BlockSpec(memory_space=pl.ANY)]
    args = [blk_expert, nxt, n_used, x_rows, w_gate_up, w_down]
    aliases = {}
    out_rows = grid_blocks * tm
    if out_prev is not None:
        in_specs.append(pl.BlockSpec(memory_space=pl.ANY))
        aliases = {len(args): 0}
        args.append(out_prev)
        out_rows = out_prev.shape[0]
    grid_spec = pltpu.PrefetchScalarGridSpec(
        num_scalar_prefetch=3,
        grid=(grid_blocks,),
        in_specs=in_specs,
        out_specs=pl.BlockSpec((tm, d_out), lambda i, be, nx, nu: (i + out_block0, 0)),
        scratch_shapes=[pltpu.VMEM((2, D, F2), F32), pltpu.VMEM((2, F2 // 2, D), F32),
                        pltpu.VMEM((D, F2), BF16), pltpu.VMEM((F2 // 2, D), BF16),
                        pltpu.SemaphoreType.DMA((2, 2)), pltpu.SMEM((1,), jnp.int32)],
    )
    return pl.pallas_call(
        functools.partial(_expert_kernel, layer=layer),
        grid_spec=grid_spec,
        out_shape=jax.ShapeDtypeStruct((out_rows, d_out), out_dtype),
        input_output_aliases=aliases,
        compiler_params=pltpu.CompilerParams(
            dimension_semantics=("arbitrary",),
            vmem_limit_bytes=_vmem_limit(need + (4 << 20))),
        name=name,
    )(*args)


def _combine_kernel(x_ref, sh_ref, y_ref, gate_ref, g_ref, b_ref, o_ref, o16_ref):
    half = x_ref.shape[-1] // 2
    acc = ALPHA * x_ref[...] + sh_ref[...]
    acc_lo, acc_hi = acc[:, :half], acc[:, half:]
    gate = gate_ref[...]
    for k in range(TOP_K):
        y_lo, y_hi = _unpack_halves(y_ref[k])
        acc_lo = acc_lo + gate[:, k:k + 1] * y_lo
        acc_hi = acc_hi + gate[:, k:k + 1] * y_hi
    out = _layer_norm_rows(jnp.concatenate([acc_lo, acc_hi], axis=1), g_ref[...], b_ref[...])
    o_ref[...] = out
    o16_ref[...] = out.astype(BF16)


def _moe_combine(x, shared, y_tok, gate, ln_g, ln_b, tm=256):
    T, D = x.shape
    tm = min(tm, T)
    assert T % tm == 0
    need = 2 * (3 * _nbytes((tm, D), F32) + _nbytes((tm, D), BF16) + _nbytes((tm, TOP_K * D), BF16)) \
        + 3 * _nbytes((tm, D), F32)
    return pl.pallas_call(
        _combine_kernel,
        grid=(T // tm,),
        in_specs=[pl.BlockSpec((tm, D), lambda i: (i, 0)),
                  pl.BlockSpec((tm, D), lambda i: (i, 0)),
                  pl.BlockSpec((TOP_K, tm, D // 2), lambda i: (0, i, 0)),
                  pl.BlockSpec((tm, TOP_K), lambda i: (i, 0)),
                  pl.BlockSpec((1, D), lambda i: (0, 0)),
                  pl.BlockSpec((1, D), lambda i: (0, 0))],
        out_specs=[pl.BlockSpec((tm, D), lambda i: (i, 0)),
                   pl.BlockSpec((tm, D), lambda i: (i, 0))],
        out_shape=[jax.ShapeDtypeStruct((T, D), F32), jax.ShapeDtypeStruct((T, D), BF16)],
        compiler_params=pltpu.CompilerParams(
            dimension_semantics=("parallel",),
            vmem_limit_bytes=_vmem_limit(need + (4 << 20))),
        name="moe_combine",
    )(x, shared, y_tok, gate, ln_g.reshape(1, D), ln_b.reshape(1, D))


MLA_QK_PAD = 2 * MLA_NOPE_DIM
MLA_IN_PAD = MLA_Q_RANK + MLA_KV_RANK + 2 * MLA_NOPE_DIM


def _mla_proj_kernel(x_ref, win_ref, wq_ref, wkv_ref, qg_ref, kvg_ref, csq_ref, ck_ref, sk_ref,
                     q_ref, k_ref, v_ref):
    NP = MLA_NOPE_DIM
    c = jnp.dot(x_ref[...], win_ref[...], preferred_element_type=F32)

    def rms(t, g):
        return (t * lax.rsqrt(jnp.mean(t * t, axis=-1, keepdims=True) + RMS_EPS) * g).astype(BF16)

    cq = rms(c[:, :MLA_Q_RANK], qg_ref[...])
    ckv = rms(c[:, MLA_Q_RANK:MLA_Q_RANK + MLA_KV_RANK], kvg_ref[...])
    o = MLA_Q_RANK + MLA_KV_RANK
    k_rope = (c[:, o:o + NP] * ck_ref[...] + c[:, o + NP:o + 2 * NP] * sk_ref[...]).astype(BF16)
    low = lax.broadcasted_iota(jnp.int32, (x_ref.shape[0], NP), 1) < MLA_ROPE_DIM
    csq = csq_ref[...]
    for h in range(MLA_HEADS):
        qh = jnp.dot(cq, wq_ref[:, h * 2 * NP:(h + 1) * 2 * NP], preferred_element_type=F32)
        u = qh[:, NP:] * csq
        q_rope = jnp.where(low, u + pltpu.roll(u, MLA_ROPE_DIM, axis=1), 0.0)
        q_ref[0, h, :, :NP] = (qh[:, :NP] * MLA_Q_SCALE).astype(BF16)
        q_ref[0, h, :, NP:] = (q_rope * MLA_Q_SCALE).astype(BF16)
        kvh = jnp.dot(ckv, wkv_ref[:, h * 2 * NP:(h + 1) * 2 * NP], preferred_element_type=F32)
        k_ref[0, h, :, :NP] = kvh[:, :NP].astype(BF16)
        k_ref[0, h, :, NP:] = k_rope
        v_ref[0, h] = kvh[:, NP:].astype(BF16)


def _mla_proj(x16, w_in_ext, w_q_ext, w_kvb, q_norm_g, kv_norm_g, csq, ck, sk, batch, tm=256):
    T, D = x16.shape
    S = T // batch
    H = MLA_HEADS
    tm = min(tm, S)
    assert S % tm == 0
    nsb = S // tm
    w_bytes = (w_in_ext.size + w_q_ext.size + w_kvb.size) * 2
    out_bytes = _nbytes((H, tm, 2 * MLA_QK_PAD + MLA_V_DIM), BF16)
    need = w_bytes + 2 * _nbytes((tm, D), BF16) + 2 * out_bytes + 4 * _nbytes((tm, MLA_IN_PAD), F32)

    def const(shape):
        return pl.BlockSpec(shape, lambda i: (0, 0), pipeline_mode=pl.Buffered(1))

    def rows(width):
        return pl.BlockSpec((tm, width), lambda i: (i, 0))

    def heads(width):
        return pl.BlockSpec((1, H, tm, width), lambda i: (i // nsb, 0, i % nsb, 0))

    return pl.pallas_call(
        _mla_proj_kernel,
        grid=(T // tm,),
        in_specs=[rows(D), const(w_in_ext.shape), const(w_q_ext.shape), const(w_kvb.shape),
                  const((1, MLA_Q_RANK)), const((1, MLA_KV_RANK)),
                  rows(MLA_NOPE_DIM), rows(MLA_NOPE_DIM), rows(MLA_NOPE_DIM)],
        out_specs=[heads(MLA_QK_PAD), heads(MLA_QK_PAD), heads(MLA_V_DIM)],
        out_shape=[jax.ShapeDtypeStruct((batch, H, S, MLA_QK_PAD), BF16),
                   jax.ShapeDtypeStruct((batch, H, S, MLA_QK_PAD), BF16),
                   jax.ShapeDtypeStruct((batch, H, S, MLA_V_DIM), BF16)],
        compiler_params=pltpu.CompilerParams(
            dimension_semantics=("parallel",),
            vmem_limit_bytes=_vmem_limit(need + (6 << 20))),
        name="mla_proj",
    )(x16, w_in_ext, w_q_ext, w_kvb, q_norm_g.reshape(1, -1), kv_norm_g.reshape(1, -1), csq, ck, sk)


def _router_kernel(x_ref, wt_ref, bias_ref, e_ref, gate_ref, rank_ref, cnt_ref, carry_ref):
    E, G, K = N_EXPERTS, N_GROUPS, TOP_K
    P = E // G
    tm = x_ref.shape[0]

    @pl.when(pl.program_id(0) == 0)
    def _():
        carry_ref[...] = jnp.zeros_like(carry_ref)

    logits = lax.dot_general(wt_ref[...], x_ref[...], (((1,), (1,)), ((), ())),
                             precision=lax.Precision.HIGHEST, preferred_element_type=F32)
    scores = jax.nn.sigmoid(logits)
    biased = scores + bias_ref[...]

    def preceding(v):
        n = v.shape[0]
        idx = lax.broadcasted_iota(jnp.int32, (n, 1), 0)
        cnt = jnp.zeros(v.shape, F32)
        for j in range(n):
            row = v[j:j + 1, :]
            tie = jnp.where(idx > j, 1.0, 0.0)
            cnt = cnt + jnp.where(row > v, 1.0, jnp.where(row == v, tie, 0.0))
        return cnt

    blocks = [biased[g * P:(g + 1) * P, :] for g in range(G)]
    group_score = jnp.concatenate(
        [jnp.sum(jnp.where(preceding(b) < 2.0, b, 0.0), axis=0, keepdims=True) for b in blocks], axis=0)
    group_sel = preceding(group_score) < float(TOPK_GROUPS)
    masked = jnp.concatenate(
        [jnp.where(group_sel[g:g + 1, :], blocks[g], -jnp.inf) for g in range(G)], axis=0)
    sel = preceding(masked) < float(K)
    sel_f = jnp.where(sel, 1.0, 0.0)
    gate = jnp.where(sel, scores, 0.0)
    gate = gate / (jnp.sum(gate, axis=0, keepdims=True) + 1e-20) * ROUTED_SCALE

    t_row = lax.broadcasted_iota(jnp.int32, (tm, tm), 0)
    t_col = lax.broadcasted_iota(jnp.int32, (tm, tm), 1)
    before_t = jnp.where(t_row < t_col, 1.0, 0.0).astype(BF16)
    sel16 = sel_f.astype(BF16)
    rank = jnp.dot(sel16, before_t, preferred_element_type=F32) + carry_ref[...]
    carry_ref[...] = carry_ref[...] + jnp.sum(sel_f, axis=1, keepdims=True)
    e_row = lax.broadcasted_iota(jnp.int32, (E, E), 0)
    e_col = lax.broadcasted_iota(jnp.int32, (E, E), 1)
    before_e = jnp.where(e_col < e_row, 1.0, 0.0).astype(BF16)
    slot = jnp.where(sel, jnp.dot(before_e, sel16, preferred_element_type=F32), -1.0)
    e_idx = lax.broadcasted_iota(jnp.int32, (E, 1), 0).astype(F32)

    def pick(k, t):
        return jnp.sum(jnp.where(slot == float(k), t, 0.0), axis=0, keepdims=True)

    e_ref[...] = jnp.concatenate([pick(k, e_idx) for k in range(K)], axis=0).astype(jnp.int32)
    rank_ref[...] = jnp.concatenate([pick(k, rank) for k in range(K)], axis=0).astype(jnp.int32)
    gate_ref[...] = jnp.concatenate([pick(k, gate) for k in range(K)], axis=0)
    cnt_ref[...] = jnp.broadcast_to(carry_ref[...], cnt_ref.shape).astype(jnp.int32)


def _router(x, router_wt, router_bias, tm=512):
    T, D = x.shape
    E = router_wt.shape[0]
    tm = min(tm, T)
    assert T % tm == 0
    kt = pl.BlockSpec((TOP_K, tm), lambda i: (0, i))
    top_e, gate, rank, cnt = pl.pallas_call(
        _router_kernel,
        grid=(T // tm,),
        in_specs=[pl.BlockSpec((tm, D), lambda i: (i, 0)),
                  pl.BlockSpec((E, D), lambda i: (0, 0)),
                  pl.BlockSpec((E, 1), lambda i: (0, 0))],
        out_specs=[kt, kt, kt, pl.BlockSpec((E, 128), lambda i: (0, 0))],
        out_shape=[jax.ShapeDtypeStruct((TOP_K, T), jnp.int32), jax.ShapeDtypeStruct((TOP_K, T), F32),
                   jax.ShapeDtypeStruct((TOP_K, T), jnp.int32), jax.ShapeDtypeStruct((E, 128), jnp.int32)],
        scratch_shapes=[pltpu.VMEM((E, 1), F32)],
        compiler_params=pltpu.CompilerParams(
            dimension_semantics=("arbitrary",),
            vmem_limit_bytes=_vmem_limit(2 * _nbytes((tm, D), F32) + (16 << 20))),
        name="moe_router",
    )(x, router_wt, router_bias.reshape(E, 1))
    return top_e, gate, rank, cnt[:, 0]


def _mla_mixer(x, x16, positions, w_in, q_norm_g, kv_norm_g, w_qb, w_kvb, w_o, ln_g, ln_b):
    B, S = positions.shape
    T, D = x.shape
    H = MLA_HEADS
    half = MLA_ROPE_DIM // 2

    def rot(w):
        return jnp.concatenate([-w[..., half:], w[..., :half]], axis=-1)

    lat = MLA_Q_RANK + MLA_KV_RANK
    w_kr = w_in[:, lat:]
    zpad = jnp.zeros((D, MLA_NOPE_DIM - MLA_ROPE_DIM), w_in.dtype)
    w_in_ext = jnp.concatenate([w_in[:, :lat], w_kr, zpad, rot(w_kr), zpad], axis=1).astype(BF16)
    w_q3 = w_qb.reshape(MLA_Q_RANK, H, MLA_QK_DIM)
    w_q_rope = w_q3[..., MLA_NOPE_DIM:]
    w_q_ext = jnp.concatenate([w_q3[..., :MLA_NOPE_DIM], w_q_rope, rot(w_q_rope)], axis=-1)
    w_q_ext = w_q_ext.reshape(MLA_Q_RANK, H * MLA_QK_PAD).astype(BF16)
    inv_freq = 1.0 / (ROPE_THETA ** (jnp.arange(0, MLA_ROPE_DIM, 2, dtype=F32) / MLA_ROPE_DIM))
    ang = positions.astype(F32).reshape(T, 1) * inv_freq
    cos, sin = jnp.cos(ang), jnp.sin(ang)
    zero = jnp.zeros_like(cos)
    csq = jnp.concatenate([cos, cos, sin, sin], axis=1)
    ck = jnp.concatenate([cos, cos, zero, zero], axis=1)
    sk = jnp.concatenate([sin, sin, zero, zero], axis=1)
    q_cat, k_cat, v = _mla_proj(x16, w_in_ext, w_q_ext, w_kvb.astype(BF16), q_norm_g, kv_norm_g,
                                csq, ck, sk, B)
    o = _attention(q_cat, k_cat, v)
    return _matmul_ln(o.reshape(T, H * MLA_V_DIM), w_o.astype(BF16), x, ln_g, ln_b, 512, "mla_out_ln")


def _gdn_mixer(x, x16, batch, w_in, conv_w, a_log, dt_bias, norm_g, w_out, ln_g, ln_b):
    T, D = x.shape
    B, S = batch, T // batch
    n_main = GDN_CONV_DIM + GDN_V_WIDTH
    p = _matmul(x16, w_in[:, :n_main].astype(BF16), F32, name="gdn_in")
    ba = _matmul(x16, w_in[:, n_main:].astype(BF16), F32, name="gdn_in_ba")
    b = ba[:, :GDN_V_HEADS].reshape(B, S, GDN_V_HEADS)
    a = ba[:, GDN_V_HEADS:].reshape(B, S, GDN_V_HEADS)
    beta = jax.nn.sigmoid(b)
    g = -jnp.exp(a_log) * jax.nn.softplus(a + dt_bias)
    N = S // CHUNK
    g_cum = jnp.cumsum(g.reshape(B, N, CHUNK, GDN_V_HEADS), axis=2)
    g_cum = jnp.transpose(g_cum, (0, 3, 1, 2))
    beta = jnp.transpose(beta.reshape(B, N, CHUNK, GDN_V_HEADS), (0, 3, 1, 2))
    o = _gated_delta_rule(p.reshape(B, S, n_main), conv_w, g_cum, beta, norm_g.reshape(1, GDN_V_DIM))
    return _matmul_ln(o.reshape(T, GDN_V_WIDTH), w_out.astype(BF16), x, ln_g, ln_b, 256, "gdn_out_ln")


def _moe_ffn(x, xpk, layer, router_w, router_bias, w_gate_up, w_down, shared_gate_up, shared_down, ln_g, ln_b,
             y_buffer):
    T, D = x.shape
    E, K = N_EXPERTS, TOP_K
    top_e, gate, rank, counts = _router(x, router_w.T, router_bias)

    zero = jnp.zeros((1,), jnp.int32)
    tm_sh = min(512, T)
    shared = _grouped_swiglu(xpk, shared_gate_up, shared_down, layer,
                             jnp.zeros((T // tm_sh,), jnp.int32), zero + T // tm_sh, tm_sh, F32,
                             "shared_expert")

    n_assign = T * K
    n_blocks = -(-n_assign // MOE_BLOCK) + E
    starts = jnp.cumsum(counts) - counts
    padded = (counts + MOE_BLOCK - 1) // MOE_BLOCK * MOE_BLOCK
    pad_ends = jnp.cumsum(padded)
    pad_starts = pad_ends - padded
    e_ids = jnp.arange(E, dtype=jnp.int32)
    pos = jnp.sum(jnp.where(top_e[..., None] == e_ids, pad_starts, 0), axis=-1) + rank
    blk_first = jnp.arange(n_blocks, dtype=jnp.int32) * MOE_BLOCK
    blk_expert = jnp.minimum(jnp.sum((pad_ends[None, :] <= blk_first[:, None]).astype(jnp.int32), axis=1),
                             E - 1)
    n_used = (pad_ends[-1:] // MOE_BLOCK).astype(jnp.int32)
    tok = jnp.arange(T, dtype=jnp.int32)
    order = jnp.sort((top_e * T + tok[None, :]).reshape(n_assign)) % T
    row_j = (blk_first - pad_starts[blk_expert])[:, None] + jnp.arange(MOE_BLOCK, dtype=jnp.int32)[None, :]
    row_src = jnp.clip(starts[blk_expert][:, None] + row_j, 0, n_assign - 1)
    pad_tok = (blk_first[:, None] + jnp.arange(MOE_BLOCK, dtype=jnp.int32)[None, :]) % T
    row_tok = jnp.where(row_j < counts[blk_expert][:, None], order[row_src], pad_tok)

    n_chunks = MOE_GATHER_CHUNKS if n_blocks % MOE_GATHER_CHUNKS == 0 else 1
    cb = n_blocks // n_chunks
    y_rows = y_buffer
    for c in range(n_chunks):
        x_rows = xpk.at[row_tok[c * cb:(c + 1) * cb].reshape(-1)].get(mode="promise_in_bounds")
        y_rows = _grouped_swiglu(x_rows, w_gate_up, w_down, layer, blk_expert[c * cb:(c + 1) * cb],
                                 jnp.clip(n_used - c * cb, 0, cb), MOE_BLOCK, jnp.uint32, "routed_experts",
                                 out_prev=y_rows, out_block0=c * cb,
                                 total_blocks=n_blocks if y_rows is None else None)
    y_tok = y_rows.at[pos.reshape(n_assign)].get(mode="promise_in_bounds").reshape(K, T, D // 2)
    x_new, x_new16 = _moe_combine(x, shared, y_tok, gate.T, ln_g, ln_b)
    return x_new, x_new16, y_rows


def kernel(x, positions, mla_w_in, mla_q_norm, mla_kv_norm, mla_w_qb, mla_w_kvb, mla_w_o, gdn_w_in, gdn_conv_w, gdn_a_log, gdn_dt_bias, gdn_norm, gdn_w_out, ln_mix_g, ln_mix_b, ln_ffn_g, ln_ffn_b, moe_router, moe_router_bias, moe_w_gate_up, moe_w_down, moe_shared_gate_up, moe_shared_down):
    B, S, D = x.shape
    x = x.reshape(B * S, D)
    x16 = x.astype(BF16)
    y_buffer = None
    for i in range(DEPTH):
        j = i // 2
        if i % 2 == 0:
            x, xpk = _mla_mixer(x, x16, positions, mla_w_in[j], mla_q_norm[j], mla_kv_norm[j],
                                mla_w_qb[j], mla_w_kvb[j], mla_w_o[j], ln_mix_g[i], ln_mix_b[i])
        else:
            x, xpk = _gdn_mixer(x, x16, B, gdn_w_in[j], gdn_conv_w[j], gdn_a_log[j], gdn_dt_bias[j],
                                gdn_norm[j], gdn_w_out[j], ln_mix_g[i], ln_mix_b[i])
        x, x16, y_buffer = _moe_ffn(x, xpk, i, moe_router[i], moe_router_bias[i], moe_w_gate_up, moe_w_down,
                                    moe_shared_gate_up[:, None], moe_shared_down[:, None],
                                    ln_ffn_g[i], ln_ffn_b[i], y_buffer)
    return x.reshape(B, S, D)
```
